```python
import math
import jax, jax.numpy as jnp
from jax import lax
import numpy as np

D_MODEL = 2048
BATCH = 4
SEQ = 4096
DEPTH = 4

N_A_LAYERS = DEPTH // 2
N_B_LAYERS = DEPTH - N_A_LAYERS
EPS = 1e-6

SSM_EXPAND = 2
D_INNER = SSM_EXPAND * D_MODEL
SSM_HEADDIM = 64
SSM_HEADS = D_INNER // SSM_HEADDIM
SSM_GROUPS = 8
SSM_STATE = 128
CONV_WIDTH = 4
CHUNK = 128
CONV_DIM = D_INNER + 2 * SSM_GROUPS * SSM_STATE
IN_PROJ_DIM = D_INNER + CONV_DIM + SSM_HEADS

MLA_HEADS = 16
Q_LORA = 512
KV_LORA = 512
QK_NOPE = 128
QK_ROPE = 64
V_HEAD = 128
ROPE_THETA = 10000.0
Q_BLOCK = 128

N_GROUPS = 4
EXPERTS_PER_GROUP = 8
N_EXPERTS = N_GROUPS * EXPERTS_PER_GROUP
TOP_K = 2
D_EXPERT = 512
DISPATCH_BLOCK = 128

kernel_name = 'ssd_yoco_mla_hier_moe_trunk'


def rmsnorm(x, g):
    xf = x.astype(jnp.float32)
    y = xf * lax.rsqrt(jnp.mean(xf * xf, axis=-1, keepdims=True) + EPS)
    return (y * g.astype(jnp.float32)).astype(x.dtype)


def grouped_rmsnorm(y, g):
    shp = y.shape
    yf = y.astype(jnp.float32).reshape(*shp[:-1], SSM_GROUPS, shp[-1] // SSM_GROUPS)
    yf = yf * lax.rsqrt(jnp.mean(yf * yf, axis=-1, keepdims=True) + EPS)
    return yf.reshape(shp) * g.astype(jnp.float32)


def causal_depthwise_conv(u, w, b):
    out = lax.conv_general_dilated(u, w[:, None, :].astype(u.dtype), window_strides=(1,),
                                   padding=[(CONV_WIDTH - 1, 0)],
                                   dimension_numbers=('NWC', 'WIO', 'NWC'),
                                   feature_group_count=u.shape[-1])
    return out + b.astype(u.dtype)


def ssd_chunked(X, dt, A, Bm, Cm):
    bsz, seqlen = X.shape[:2]
    nc = seqlen // CHUNK
    r = SSM_HEADS // SSM_GROUPS
    Xd = (X.astype(jnp.float32) * dt[..., None]).reshape(bsz, nc, CHUNK, SSM_GROUPS, r, SSM_HEADDIM)
    a = (dt * A).reshape(bsz, nc, CHUNK, SSM_GROUPS, r).transpose(0, 1, 3, 4, 2)
    cs = jnp.cumsum(a, axis=-1)
    Bc = Bm.astype(jnp.float32).reshape(bsz, nc, CHUNK, SSM_GROUPS, SSM_STATE)
    Cc = Cm.astype(jnp.float32).reshape(bsz, nc, CHUNK, SSM_GROUPS, SSM_STATE)
    causal = jnp.tril(jnp.ones((CHUNK, CHUNK), dtype=bool))
    Lmat = jnp.exp(jnp.where(causal, cs[..., :, None] - cs[..., None, :], -jnp.inf))
    CB = jnp.einsum('bclgn,bcsgn->bcgls', Cc, Bc)
    y_diag = jnp.einsum('bcgls,bcgrls,bcsgrp->bclgrp', CB, Lmat, Xd)
    decay_to_end = jnp.exp(cs[..., -1:] - cs)
    states = jnp.einsum('bcsgn,bcgrs,bcsgrp->bcgrpn', Bc, decay_to_end, Xd)
    chunk_decay = jnp.exp(cs[..., -1])

    def step(h, inp):
        dec, st = inp
        return h * dec[..., None, None] + st, h

    _, prev = lax.scan(step, jnp.zeros_like(states[:, 0]),
                       (jnp.moveaxis(chunk_decay, 1, 0), jnp.moveaxis(states, 1, 0)))
    prev = jnp.moveaxis(prev, 0, 1)
    y_off = jnp.einsum('bclgn,bcgrpn,bcgrl->bclgrp', Cc, prev, jnp.exp(cs))
    return (y_diag + y_off).reshape(bsz, seqlen, SSM_HEADS, SSM_HEADDIM)


def mamba2_mixer(h, in_w, conv_w, conv_b, dt_bias, A_log, D_skip, gate_norm_g, out_w):
    bsz, seqlen, _ = h.shape
    zxbcdt = h @ in_w
    z, xbc, dt_raw = jnp.split(zxbcdt, [D_INNER, D_INNER + CONV_DIM], axis=-1)
    xbc = jax.nn.silu(causal_depthwise_conv(xbc, conv_w, conv_b))
    xs, Bm, Cm = jnp.split(xbc, [D_INNER, D_INNER + SSM_GROUPS * SSM_STATE], axis=-1)
    dt = jax.nn.softplus((dt_raw + dt_bias).astype(jnp.float32))
    A = -jnp.exp(A_log.astype(jnp.float32))
    X = xs.reshape(bsz, seqlen, SSM_HEADS, SSM_HEADDIM)
    y = ssd_chunked(X, dt, A,
                    Bm.reshape(bsz, seqlen, SSM_GROUPS, SSM_STATE),
                    Cm.reshape(bsz, seqlen, SSM_GROUPS, SSM_STATE))
    y = y + X.astype(jnp.float32) * D_skip.astype(jnp.float32)[:, None]
    y = y.reshape(bsz, seqlen, D_INNER) * jax.nn.silu(z.astype(jnp.float32))
    y = grouped_rmsnorm(y, gate_norm_g).astype(h.dtype)
    return y @ out_w


def rope_tables(seqlen):
    inv_freq = ROPE_THETA ** (-jnp.arange(0, QK_ROPE, 2, dtype=jnp.float32) / QK_ROPE)
    ang = jnp.arange(seqlen, dtype=jnp.float32)[:, None] * inv_freq[None, :]
    return jnp.cos(ang), jnp.sin(ang)


def apply_rope(t, cos, sin):
    t1, t2 = jnp.split(t, 2, axis=-1)
    return jnp.concatenate([t1 * cos - t2 * sin, t2 * cos + t1 * sin], axis=-1).astype(t.dtype)


def shared_latent_kv(x, kv_norm_g, w_dkv, kv_latent_g, w_uk, w_uv, w_kr, cos, sin):
    bsz, seqlen, _ = x.shape
    h = rmsnorm(x, kv_norm_g)
    c_kv = rmsnorm(h @ w_dkv, kv_latent_g)
    k_nope = (c_kv @ w_uk).reshape(bsz, seqlen, MLA_HEADS, QK_NOPE)
    v = (c_kv @ w_uv).reshape(bsz, seqlen, MLA_HEADS, V_HEAD)
    k_rope = apply_rope(h @ w_kr, cos, sin)
    return k_nope, k_rope, v


def mla_attention(h, k_nope, k_rope, v, w_dq, q_latent_g, w_uq, w_o, cos, sin):
    bsz, seqlen, _ = h.shape
    nb = seqlen // Q_BLOCK
    c_q = rmsnorm(h @ w_dq, q_latent_g)
    q = (c_q @ w_uq).reshape(bsz, seqlen, MLA_HEADS, QK_NOPE + QK_ROPE)
    q_nope, q_rope = jnp.split(q, [QK_NOPE], axis=-1)
    q_rope = apply_rope(q_rope, cos[:, None, :], sin[:, None, :])
    scale = (QK_NOPE + QK_ROPE) ** -0.5
    key_pos = jnp.arange(seqlen)

    def to_blocks(t):
        return jnp.moveaxis(t.reshape(bsz, nb, Q_BLOCK, *t.shape[2:]), 1, 0)

    def attend_block(args):
        blk, qn, qr = args
        s = jnp.einsum('bqhd,bkhd->bhqk', qn, k_nope) + jnp.einsum('bqhd,bkd->bhqk', qr, k_rope)
        q_pos = blk * Q_BLOCK + jnp.arange(Q_BLOCK)
        s = jnp.where(key_pos[None, :] <= q_pos[:, None], s.astype(jnp.float32) * scale, -jnp.inf)
        p = jax.nn.softmax(s, axis=-1).astype(v.dtype)
        return jnp.einsum('bhqk,bkhd->bqhd', p, v)

    o = lax.map(attend_block, (jnp.arange(nb), to_blocks(q_nope), to_blocks(q_rope)))
    o = jnp.moveaxis(o, 0, 1).reshape(bsz, seqlen, MLA_HEADS * V_HEAD)
    return o @ w_o


def grouped_expert_ffn(t, expert, gate, w_gate, w_up, w_down):
    n_tok, d = t.shape
    n_asg = expert.shape[0]
    tok = jnp.arange(n_asg) // TOP_K
    order = jnp.argsort(expert)
    se, stok, sw = expert[order], tok[order], gate[order]
    counts = jnp.bincount(expert, length=N_EXPERTS)
    starts = jnp.cumsum(counts) - counts
    padded = (counts + DISPATCH_BLOCK - 1) // DISPATCH_BLOCK * DISPATCH_BLOCK
    pends = jnp.cumsum(padded)
    pstarts = pends - padded
    pos = pstarts[se] + (jnp.arange(n_asg) - starts[se])
    n_blocks = -(-n_asg // DISPATCH_BLOCK) + N_EXPERTS
    n_rows = n_blocks * DISPATCH_BLOCK
    pad_tok = jnp.full((n_rows,), n_tok, dtype=jnp.int32).at[pos].set(stok.astype(jnp.int32))
    pad_w = jnp.zeros((n_rows,), dtype=gate.dtype).at[pos].set(sw)
    block_expert = jnp.minimum(jnp.searchsorted(pends, jnp.arange(n_blocks) * DISPATCH_BLOCK, side='right'),
                               N_EXPERTS - 1)
    t_pad = jnp.concatenate([t, jnp.zeros((1, d), t.dtype)], axis=0)
    xb = t_pad[pad_tok].reshape(n_blocks, DISPATCH_BLOCK, d)

    def expert_block(args):
        xblk, e = args
        hid = jax.nn.silu(xblk @ w_gate[e]) * (xblk @ w_up[e])
        return hid @ w_down[e]

    yb = lax.map(expert_block, (xb, block_expert)).reshape(n_rows, d)
    y = jax.ops.segment_sum(yb * pad_w[:, None], pad_tok, num_segments=n_tok + 1)[:n_tok]
    return y.astype(t.dtype)


def hier_moe(h, rg_w, rg_b, re_w, re_b, w_gate, w_up, w_down):
    bsz, seqlen, d = h.shape
    t = h.reshape(bsz * seqlen, d)
    n_tok = t.shape[0]
    g_prob = jax.nn.softmax((t @ rg_w).astype(jnp.float32) + rg_b, axis=-1)
    g_w, g_idx = lax.top_k(g_prob, 1)
    e_all = jnp.einsum('td,gde->tge', t, re_w).astype(jnp.float32) + re_b
    e_logits = e_all[jnp.arange(n_tok), g_idx[:, 0]]
    e_w, e_idx = lax.top_k(jax.nn.softmax(e_logits, axis=-1), TOP_K)
    e_w = e_w / jnp.sum(e_w, axis=-1, keepdims=True)
    gate = (g_w * e_w).reshape(-1)
    expert = (g_idx * EXPERTS_PER_GROUP + e_idx).reshape(-1)
    return grouped_expert_ffn(t, expert, gate, w_gate, w_up, w_down).reshape(bsz, seqlen, d)


def setup_inputs(seed: int = 0) -> dict:
    key = jax.random.key(seed)
    ks = iter(jax.random.split(key, 48))
    f32 = jnp.float32

    def normal(shape, scale):
        return jax.random.normal(next(ks), shape, f32) * scale

    def gain(shape):
        return 1.0 + normal(shape, 0.02)

    NA, NB = N_A_LAYERS, N_B_LAYERS
    x = normal((BATCH, SEQ, D_MODEL), 1.0)
    ssm_norm_g = gain((NA, D_MODEL))
    ssm_in_w = normal((NA, D_MODEL, IN_PROJ_DIM), D_MODEL ** -0.5)
    ssm_conv_w = normal((NA, CONV_WIDTH, CONV_DIM), CONV_WIDTH ** -0.5)
    ssm_conv_b = normal((NA, CONV_DIM), 0.02)
    dt0 = jnp.exp(jax.random.uniform(next(ks), (NA, SSM_HEADS), f32, math.log(1e-3), math.log(1e-1)))
    ssm_dt_bias = dt0 + jnp.log(-jnp.expm1(-dt0))
    ssm_A_log = jnp.log(jax.random.uniform(next(ks), (NA, SSM_HEADS), f32, 1.0, 16.0))
    ssm_D = gain((NA, SSM_HEADS))
    ssm_gate_norm_g = gain((NA, D_INNER))
    ssm_out_w = normal((NA, D_INNER, D_MODEL), D_INNER ** -0.5)
    kv_norm_g = gain((D_MODEL,))
    kv_w_dkv = normal((D_MODEL, KV_LORA), D_MODEL ** -0.5)
    kv_latent_g = gain((KV_LORA,))
    kv_w_uk = normal((KV_LORA, MLA_HEADS * QK_NOPE), KV_LORA ** -0.5)
    kv_w_uv = normal((KV_LORA, MLA_HEADS * V_HEAD), KV_LORA ** -0.5)
    kv_w_kr = normal((D_MODEL, QK_ROPE), D_MODEL ** -0.5)
    attn_norm_g = gain((NB, D_MODEL))
    q_w_dq = normal((NB, D_MODEL, Q_LORA), D_MODEL ** -0.5)
    q_latent_g = gain((NB, Q_LORA))
    q_w_uq = normal((NB, Q_LORA, MLA_HEADS * (QK_NOPE + QK_ROPE)), Q_LORA ** -0.5)
    attn_w_o = normal((NB, MLA_HEADS * V_HEAD, D_MODEL), (MLA_HEADS * V_HEAD) ** -0.5)
    ffn_norm_g = gain((DEPTH, D_MODEL))
    router_group_w = normal((DEPTH, D_MODEL, N_GROUPS), D_MODEL ** -0.5)
    router_group_b = normal((DEPTH, N_GROUPS), 0.01)
    router_expert_w = normal((DEPTH, N_GROUPS, D_MODEL, EXPERTS_PER_GROUP), D_MODEL ** -0.5)
    router_expert_b = normal((DEPTH, N_GROUPS, EXPERTS_PER_GROUP), 0.01)
    expert_w_gate = normal((DEPTH, N_EXPERTS, D_MODEL, D_EXPERT), D_MODEL ** -0.5)
    expert_w_up = normal((DEPTH, N_EXPERTS, D_MODEL, D_EXPERT), D_MODEL ** -0.5)
    expert_w_down = normal((DEPTH, N_EXPERTS, D_EXPERT, D_MODEL), D_EXPERT ** -0.5)
    final_norm_g = gain((D_MODEL,))
    return {'x': x, 'ssm_norm_g': ssm_norm_g, 'ssm_in_w': ssm_in_w, 'ssm_conv_w': ssm_conv_w,
            'ssm_conv_b': ssm_conv_b, 'ssm_dt_bias': ssm_dt_bias, 'ssm_A_log': ssm_A_log, 'ssm_D': ssm_D,
            'ssm_gate_norm_g': ssm_gate_norm_g, 'ssm_out_w': ssm_out_w,
            'kv_norm_g': kv_norm_g, 'kv_w_dkv': kv_w_dkv, 'kv_latent_g': kv_latent_g, 'kv_w_uk': kv_w_uk,
            'kv_w_uv': kv_w_uv, 'kv_w_kr': kv_w_kr,
            'attn_norm_g': attn_norm_g, 'q_w_dq': q_w_dq, 'q_latent_g': q_latent_g, 'q_w_uq': q_w_uq,
            'attn_w_o': attn_w_o,
            'ffn_norm_g': ffn_norm_g, 'router_group_w': router_group_w, 'router_group_b': router_group_b,
            'router_expert_w': router_expert_w, 'router_expert_b': router_expert_b,
            'expert_w_gate': expert_w_gate, 'expert_w_up': expert_w_up, 'expert_w_down': expert_w_down,
            'final_norm_g': final_norm_g}


def reference(x, ssm_norm_g, ssm_in_w, ssm_conv_w, ssm_conv_b, ssm_dt_bias, ssm_A_log, ssm_D,
              ssm_gate_norm_g, ssm_out_w, kv_norm_g, kv_w_dkv, kv_latent_g, kv_w_uk, kv_w_uv, kv_w_kr,
              attn_norm_g, q_w_dq, q_latent_g, q_w_uq, attn_w_o, ffn_norm_g, router_group_w,
              router_group_b, router_expert_w, router_expert_b, expert_w_gate, expert_w_up,
              expert_w_down, final_norm_g):
    cos, sin = rope_tables(x.shape[1])
    k_nope = k_rope = v = None
    for layer in range(DEPTH):
        if layer < N_A_LAYERS:
            i = layer
            h = rmsnorm(x, ssm_norm_g[i])
            x = x + mamba2_mixer(h, ssm_in_w[i], ssm_conv_w[i], ssm_conv_b[i], ssm_dt_bias[i],
                                 ssm_A_log[i], ssm_D[i], ssm_gate_norm_g[i], ssm_out_w[i])
        else:
            if layer == N_A_LAYERS:
                k_nope, k_rope, v = shared_latent_kv(x, kv_norm_g, kv_w_dkv, kv_latent_g, kv_w_uk, kv_w_uv,
                                                     kv_w_kr, cos, sin)
            j = layer - N_A_LAYERS
            h = rmsnorm(x, attn_norm_g[j])
            x = x + mla_attention(h, k_nope, k_rope, v, q_w_dq[j], q_latent_g[j], q_w_uq[j], attn_w_o[j],
                                  cos, sin)
        h = rmsnorm(x, ffn_norm_g[layer])
        x = x + hier_moe(h, router_group_w[layer], router_group_b[layer], router_expert_w[layer],
                         router_expert_b[layer], expert_w_gate[layer], expert_w_up[layer],
                         expert_w_down[layer])
    return rmsnorm(x, final_norm_g)
```

```python
import functools
import math

import jax
import jax.numpy as jnp
from jax import lax
from jax.experimental import pallas as pl
from jax.experimental.pallas import tpu as pltpu

F32 = jnp.float32
BF16 = jnp.bfloat16
I32 = jnp.int32
HIGHEST = lax.Precision.HIGHEST

EPS = 1e-6
D_MODEL = 2048

D_INNER = 4096
SSM_HEADDIM = 64
SSM_HEADS = 64
SSM_GROUPS = 8
SSM_STATE = 128
CONV_WIDTH = 4
CHUNK = 128
CONV_DIM = D_INNER + 2 * SSM_GROUPS * SSM_STATE
GROUP_CH = D_INNER // SSM_GROUPS
CONV_HALO = 8

MLA_HEADS = 16
Q_LORA = 512
KV_LORA = 512
QK_NOPE = 128
QK_ROPE = 64
V_HEAD = 128
ROPE_THETA = 10000.0
QK_PAD = 256
ATT_TQ = 512
ATT_TK = 512

N_GROUPS = 4
EXPERTS_PER_GROUP = 8
N_EXPERTS = 32
TOP_K = 2
D_EXPERT = 512
ROUTE_LANES = 128
EXPERT_LANE0 = N_GROUPS
EXPERT_BLOCK = 256

LANES = 128
VMEM_LIMIT = 56 * 1024 * 1024


def _cparams(n_axes, **kw):
    return pltpu.CompilerParams(dimension_semantics=("arbitrary",) * n_axes,
                                vmem_limit_bytes=VMEM_LIMIT, **kw)


def _rms(xf, g):
    return xf * lax.rsqrt(jnp.mean(xf * xf, axis=-1, keepdims=True) + EPS) * g


def _dot(a, b):
    return jnp.dot(a, b, preferred_element_type=F32)


def _silu(v):
    return v * jax.nn.sigmoid(v)


def _mm_res_kernel(a_ref, w_ref, r_ref, o_ref):
    o_ref[...] = r_ref[...] + _dot(a_ref[...], w_ref[...])


def _mm_res(a, w, res, *, tm=512, tn=512):
    m, k = a.shape
    n = w.shape[1]
    return pl.pallas_call(
        _mm_res_kernel,
        grid=(m // tm, n // tn),
        in_specs=[pl.BlockSpec((tm, k), lambda i, j: (i, 0)),
                  pl.BlockSpec((k, tn), lambda i, j: (0, j)),
                  pl.BlockSpec((tm, tn), lambda i, j: (i, j))],
        out_specs=pl.BlockSpec((tm, tn), lambda i, j: (i, j)),
        out_shape=jax.ShapeDtypeStruct((m, n), F32),
        compiler_params=_cparams(2),
        name="mm_res",
    )(a, w, res)


def _in_proj_kernel(x_ref, g_ref, w_ref, wdt_ref, o_ref, odt_ref, h_scr):
    @pl.when(pl.program_id(1) == 0)
    def _():
        h = _rms(x_ref[...], g_ref[...]).astype(BF16)
        h_scr[...] = h
        odt_ref[...] = _dot(h, wdt_ref[...])

    o_ref[...] = _dot(h_scr[...], w_ref[...]).astype(o_ref.dtype)


def _in_proj(x, g, w, wdt, *, tm=1024, tn=512):
    m, k = x.shape
    n = w.shape[1] // tn * tn
    tm = min(tm, m)
    return pl.pallas_call(
        _in_proj_kernel,
        grid=(m // tm, n // tn),
        in_specs=[pl.BlockSpec((tm, k), lambda i, j: (i, 0)),
                  pl.BlockSpec((1, k), lambda i, j: (0, 0)),
                  pl.BlockSpec((k, tn), lambda i, j: (0, j)),
                  pl.BlockSpec((k, LANES), lambda i, j: (0, 0))],
        out_specs=[pl.BlockSpec((tm, tn), lambda i, j: (i, j)),
                   pl.BlockSpec((tm, LANES), lambda i, j: (i, 0))],
        out_shape=[jax.ShapeDtypeStruct((m, n), BF16),
                   jax.ShapeDtypeStruct((m, LANES), F32)],
        scratch_shapes=[pltpu.VMEM((tm, k), BF16)],
        compiler_params=_cparams(2),
        name="in_proj",
    )(x, g, w, wdt)


def _final_norm_kernel(x_ref, g_ref, o_ref):
    o_ref[...] = _rms(x_ref[...], g_ref[...])


def _final_norm(x, g, *, tm=512):
    m, k = x.shape
    return pl.pallas_call(
        _final_norm_kernel,
        grid=(m // tm,),
        in_specs=[pl.BlockSpec((tm, k), lambda i: (i, 0)),
                  pl.BlockSpec((1, k), lambda i: (0, 0))],
        out_specs=pl.BlockSpec((tm, k), lambda i: (i, 0)),
        out_shape=jax.ShapeDtypeStruct((m, k), F32),
        compiler_params=_cparams(1),
        name="final_norm",
    )(x, g)


def _softplus(v):
    return jnp.maximum(v, 0.0) + jnp.log1p(jnp.exp(-jnp.abs(v)))


def _ssd_kernel(z_ref, x_ref, bc_ref, dt_ref, cw_ref, cb_ref, dtb_ref, alog_ref, dexp_ref, gn_ref,
                y_ref, ubuf, act, xw, st):
    L = CHUNK

    @pl.when(pl.program_id(1) == 0)
    def _():
        ubuf[0:CONV_HALO, :] = jnp.zeros((CONV_HALO, CONV_DIM), F32)
        st[...] = jnp.zeros(st.shape, F32)

    ubuf[CONV_HALO:CONV_HALO + L, 0:D_INNER] = x_ref[...].astype(F32)
    ubuf[CONV_HALO:CONV_HALO + L, D_INNER:CONV_DIM] = bc_ref[...].astype(F32)
    cblk = 512
    for j in range(CONV_DIM // cblk):
        cs_ = slice(j * cblk, (j + 1) * cblk)
        acc = cb_ref[:, cs_] + cw_ref[0:1, cs_] * ubuf[CONV_HALO - 3:CONV_HALO - 3 + L, cs_]
        for k in range(1, CONV_WIDTH):
            r0 = CONV_HALO - 3 + k
            acc = acc + cw_ref[k:k + 1, cs_] * ubuf[r0:r0 + L, cs_]
        act[:, cs_] = _silu(acc)
    ubuf[0:CONV_HALO, :] = ubuf[L:L + CONV_HALO, :]

    dtv = _softplus(dt_ref[...] + dtb_ref[...])
    a = dtv * (-jnp.exp(alog_ref[...]))
    row = lax.broadcasted_iota(I32, (L, L), 0)
    col = lax.broadcasted_iota(I32, (L, L), 1)
    causal = row >= col
    cs = jnp.dot(causal.astype(F32), a, precision=HIGHEST, preferred_element_type=F32)
    cs_t = cs.T
    dt_t = dtv.T
    ecs = jnp.exp(cs)
    wgt = dtv * jnp.exp(cs[L - 1:L, :] - cs)
    first_head = col < SSM_HEADDIM

    def pair(v, h0):
        return jnp.where(first_head, v[:, h0:h0 + 1], v[:, h0 + 1:h0 + 2])

    for g in range(SSM_GROUPS):
        b0 = D_INNER + g * SSM_STATE
        c0 = D_INNER + SSM_GROUPS * SSM_STATE + g * SSM_STATE
        bg = act[:, b0:b0 + SSM_STATE].astype(BF16)
        cg = act[:, c0:c0 + SSM_STATE].astype(BF16)
        cb = lax.dot_general(cg, bg, (((1,), (1,)), ((), ())), preferred_element_type=F32)
        st_g = st[g]
        y_off = _dot(cg, st_g.astype(BF16))
        for qq in range(GROUP_CH // LANES):
            lo = g * GROUP_CH + qq * LANES
            h0 = lo // SSM_HEADDIM
            xf = act[:, lo:lo + LANES]
            xb = xf.astype(BF16)

            def head_mat(h):
                dec = jnp.exp(jnp.where(causal, cs[:, h:h + 1] - cs_t[h:h + 1, :], -jnp.inf))
                return (cb * dec * dt_t[h:h + 1, :]).astype(BF16)

            y_diag = jnp.where(first_head, _dot(head_mat(h0), xb), _dot(head_mat(h0 + 1), xb))
            e_pair = pair(ecs, h0)
            y = y_diag + y_off[:, qq * LANES:(qq + 1) * LANES] * e_pair + xf * dexp_ref[:, lo:lo + LANES]
            zf = z_ref[:, lo:lo + LANES].astype(F32)
            act[:, lo:lo + LANES] = y * _silu(zf)
            xw[:, qq * LANES:(qq + 1) * LANES] = (xf * pair(wgt, h0)).astype(BF16)
            st[g, :, qq * LANES:(qq + 1) * LANES] = (
                st_g[:, qq * LANES:(qq + 1) * LANES] * e_pair[L - 1:L, :])
        upd = lax.dot_general(bg, xw[...], (((0,), (0,)), ((), ())), preferred_element_type=F32)
        st[g] = st[g] + upd
        gs = slice(g * GROUP_CH, (g + 1) * GROUP_CH)
        y_ref[:, gs] = _rms(act[:, gs], gn_ref[:, gs]).astype(y_ref.dtype)


def _ssd(zx, dt_raw, conv_w, conv_b, dt_bias, a_log, d_exp, gate_g, *, bsz, seqlen):
    nc = seqlen // CHUNK
    row = lambda b, c: b * nc + c
    full = lambda shape: pl.BlockSpec(shape, lambda b, c: (0,) * len(shape))
    x_blk = D_INNER // D_INNER
    bc_blk = (2 * D_INNER) // (CONV_DIM - D_INNER)
    return pl.pallas_call(
        _ssd_kernel,
        grid=(bsz, nc),
        in_specs=[pl.BlockSpec((CHUNK, D_INNER), lambda b, c: (row(b, c), 0)),
                  pl.BlockSpec((CHUNK, D_INNER), lambda b, c: (row(b, c), x_blk)),
                  pl.BlockSpec((CHUNK, CONV_DIM - D_INNER), lambda b, c: (row(b, c), bc_blk)),
                  pl.BlockSpec((CHUNK, LANES), lambda b, c: (row(b, c), 0)),
                  full((CONV_WIDTH, CONV_DIM)), full((1, CONV_DIM)), full((1, LANES)), full((1, LANES)),
                  full((1, D_INNER)), full((1, D_INNER))],
        out_specs=pl.BlockSpec((CHUNK, D_INNER), lambda b, c: (row(b, c), 0)),
        out_shape=jax.ShapeDtypeStruct((bsz * seqlen, D_INNER), BF16),
        scratch_shapes=[pltpu.VMEM((CHUNK + CONV_HALO, CONV_DIM), F32),
                        pltpu.VMEM((CHUNK, CONV_DIM), F32),
                        pltpu.VMEM((CHUNK, GROUP_CH), BF16),
                        pltpu.VMEM((SSM_GROUPS, SSM_STATE, GROUP_CH), F32)],
        compiler_params=_cparams(2),
        name="ssd",
    )(zx, zx, zx, dt_raw, conv_w, conv_b, dt_bias, a_log, d_exp, gate_g)


def _kv_kernel(x_ref, g_ref, wdkv_ref, lg_ref, wuk_ref, wuv_ref, wkr_ref, ck_ref, sk_ref,
               k_ref, v_ref):
    h = _rms(x_ref[...], g_ref[...]).astype(BF16)
    ckv = _rms(_dot(h, wdkv_ref[...]), lg_ref[...]).astype(BF16)
    kn = _dot(ckv, wuk_ref[...]).astype(BF16)
    v_ref[...] = _dot(ckv, wuv_ref[...]).astype(BF16)
    kk = _dot(h, wkr_ref[...])
    kr = (kk * ck_ref[...] + pltpu.roll(kk, QK_ROPE, 1) * sk_ref[...]).astype(BF16)
    for hd in range(MLA_HEADS):
        k_ref[:, hd * QK_PAD:hd * QK_PAD + QK_NOPE] = kn[:, hd * QK_NOPE:(hd + 1) * QK_NOPE]
        k_ref[:, hd * QK_PAD + QK_NOPE:(hd + 1) * QK_PAD] = kr


def _shared_kv(x, g, wdkv, lg, wuk, wuv, wkr2, ck, sk, *, seqlen, tm=512):
    m, k = x.shape
    nt = seqlen // tm
    full = lambda a: pl.BlockSpec(a.shape, lambda i: (0,) * a.ndim)
    tab = pl.BlockSpec((tm, LANES), lambda i: (i % nt, 0))
    return pl.pallas_call(
        _kv_kernel,
        grid=(m // tm,),
        in_specs=[pl.BlockSpec((tm, k), lambda i: (i, 0)), full(g), full(wdkv), full(lg), full(wuk),
                  full(wuv), full(wkr2), tab, tab],
        out_specs=[pl.BlockSpec((tm, MLA_HEADS * QK_PAD), lambda i: (i, 0)),
                   pl.BlockSpec((tm, MLA_HEADS * V_HEAD), lambda i: (i, 0))],
        out_shape=[jax.ShapeDtypeStruct((m, MLA_HEADS * QK_PAD), BF16),
                   jax.ShapeDtypeStruct((m, MLA_HEADS * V_HEAD), BF16)],
        compiler_params=_cparams(1),
        name="shared_kv",
    )(x, g, wdkv, lg, wuk, wuv, wkr2, ck, sk)


def _q_kernel(x_ref, g_ref, wdq_ref, lg_ref, wn_ref, wr_ref, ws_ref, ck_ref, sk_ref, q_ref):
    h = _rms(x_ref[...], g_ref[...]).astype(BF16)
    cq = _rms(_dot(h, wdq_ref[...]), lg_ref[...]).astype(BF16)
    qn = _dot(cq, wn_ref[...]).astype(BF16)
    qr = _dot(cq, wr_ref[...])
    qs = _dot(cq, ws_ref[...])
    ck = ck_ref[...]
    sk = sk_ref[...]
    for hd in range(MLA_HEADS):
        q_ref[:, hd * QK_PAD:hd * QK_PAD + QK_NOPE] = qn[:, hd * QK_NOPE:(hd + 1) * QK_NOPE]
        sl = slice(hd * LANES, (hd + 1) * LANES)
        q_ref[:, hd * QK_PAD + QK_NOPE:(hd + 1) * QK_PAD] = (qr[:, sl] * ck + qs[:, sl] * sk).astype(BF16)


def _q_side(x, g, wdq, lg, wn, wr, ws, ck, sk, *, seqlen, tm=512):
    m, k = x.shape
    nt = seqlen // tm
    full = lambda a: pl.BlockSpec(a.shape, lambda i: (0,) * a.ndim)
    tab = pl.BlockSpec((tm, LANES), lambda i: (i % nt, 0))
    return pl.pallas_call(
        _q_kernel,
        grid=(m // tm,),
        in_specs=[pl.BlockSpec((tm, k), lambda i: (i, 0)), full(g), full(wdq), full(lg), full(wn),
                  full(wr), full(ws), tab, tab],
        out_specs=pl.BlockSpec((tm, MLA_HEADS * QK_PAD), lambda i: (i, 0)),
        out_shape=jax.ShapeDtypeStruct((m, MLA_HEADS * QK_PAD), BF16),
        compiler_params=_cparams(1),
        name="q_side",
    )(x, g, wdq, lg, wn, wr, ws, ck, sk)


def _attn_kernel(q_ref, k_ref, v_ref, o_ref, m_scr, l_scr, acc_scr):
    qi = pl.program_id(2)
    scale = (QK_NOPE + QK_ROPE) ** -0.5
    q = q_ref[...]
    m_scr[...] = jnp.full(m_scr.shape, -jnp.inf, F32)
    l_scr[...] = jnp.zeros(l_scr.shape, F32)
    acc_scr[...] = jnp.zeros(acc_scr.shape, F32)
    row = lax.broadcasted_iota(I32, (ATT_TQ, ATT_TK), 0)
    col = lax.broadcasted_iota(I32, (ATT_TQ, ATT_TK), 1)

    def step(kc, diagonal):
        k0 = pl.multiple_of(kc * ATT_TK, ATT_TK)
        k = k_ref[pl.ds(k0, ATT_TK), :]
        v = v_ref[pl.ds(k0, ATT_TK), :]
        s = lax.dot_general(q, k, (((1,), (1,)), ((), ())), preferred_element_type=F32) * scale
        if diagonal:
            s = jnp.where(col <= row, s, -jnp.inf)
        m_prev = m_scr[...]
        m_new = jnp.maximum(m_prev, jnp.max(s, axis=-1, keepdims=True))
        alpha = jnp.exp(m_prev - m_new)
        p = jnp.exp(s - m_new)
        l_scr[...] = alpha * l_scr[...] + jnp.sum(p, axis=-1, keepdims=True)
        acc_scr[...] = alpha * acc_scr[...] + _dot(p.astype(BF16), v)
        m_scr[...] = m_new

    def body(kc, carry):
        step(kc, False)
        return carry

    lax.fori_loop(0, qi, body, 0)
    step(qi, True)
    o_ref[...] = (acc_scr[...] / l_scr[...]).astype(o_ref.dtype)


def _attention(q, k, v, *, bsz, seqlen):
    nq = seqlen // ATT_TQ
    return pl.pallas_call(
        _attn_kernel,
        grid=(bsz, MLA_HEADS, nq),
        in_specs=[pl.BlockSpec((ATT_TQ, QK_PAD), lambda b, h, i: (b * nq + i, h)),
                  pl.BlockSpec((seqlen, QK_PAD), lambda b, h, i: (b, h)),
                  pl.BlockSpec((seqlen, V_HEAD), lambda b, h, i: (b, h))],
        out_specs=pl.BlockSpec((ATT_TQ, V_HEAD), lambda b, h, i: (b * nq + i, h)),
        out_shape=jax.ShapeDtypeStruct((bsz * seqlen, MLA_HEADS * V_HEAD), BF16),
        scratch_shapes=[pltpu.VMEM((ATT_TQ, 1), F32), pltpu.VMEM((ATT_TQ, 1), F32),
                        pltpu.VMEM((ATT_TQ, V_HEAD), F32)],
        compiler_params=_cparams(3),
        name="attention",
    )(q, k, v)


def _router_kernel(x_ref, g_ref, wr_ref, br_ref, h_ref, route_ref, cnt_ref, carry):
    tm = x_ref.shape[0]

    @pl.when(pl.program_id(0) == 0)
    def _():
        carry[...] = jnp.zeros(carry.shape, F32)

    h = _rms(x_ref[...], g_ref[...])
    h_ref[...] = h
    logits = jnp.dot(h, wr_ref[...], precision=HIGHEST, preferred_element_type=F32) + br_ref[...]
    lane = lax.broadcasted_iota(I32, (tm, ROUTE_LANES), 1)
    neg = -jnp.inf

    def first_argmax(v, vmax):
        return jnp.min(jnp.where(v == vmax, lane, ROUTE_LANES), axis=-1, keepdims=True)

    gl = jnp.where(lane < N_GROUPS, logits, neg)
    gm = jnp.max(gl, axis=-1, keepdims=True)
    g_w = 1.0 / jnp.sum(jnp.exp(gl - gm), axis=-1, keepdims=True)
    gi = first_argmax(gl, gm)
    lo = EXPERT_LANE0 + gi * EXPERTS_PER_GROUP
    el = jnp.where((lane >= lo) & (lane < lo + EXPERTS_PER_GROUP), logits, neg)
    m1 = jnp.max(el, axis=-1, keepdims=True)
    es = jnp.sum(jnp.exp(el - m1), axis=-1, keepdims=True)
    i1 = first_argmax(el, m1)
    el2 = jnp.where(lane == i1, neg, el)
    m2 = jnp.max(el2, axis=-1, keepdims=True)
    i2 = first_argmax(el2, m2)
    p1 = 1.0 / es
    p2 = jnp.exp(m2 - m1) / es
    den = p1 + p2
    w1 = g_w * (p1 / den)
    w2 = g_w * (p2 / den)
    sel1 = lane == i1
    sel2 = lane == i2
    onehot = jnp.where(sel1 | sel2, 1.0, 0.0)
    r_i = lax.broadcasted_iota(I32, (tm, tm), 0)
    c_i = lax.broadcasted_iota(I32, (tm, tm), 1)
    before = jnp.where(r_i > c_i, 1.0, 0.0).astype(BF16)
    base = carry[0:1, :] + _dot(before, onehot.astype(BF16))
    r1 = jnp.sum(jnp.where(sel1, base, 0.0), axis=-1, keepdims=True)
    r2 = jnp.sum(jnp.where(sel2, base, 0.0), axis=-1, keepdims=True)
    carry[0:1, :] = carry[0:1, :] + jnp.sum(onehot, axis=0, keepdims=True)
    cnt_ref[...] = carry[...]
    cols = [(i1 - EXPERT_LANE0).astype(F32), (i2 - EXPERT_LANE0).astype(F32), r1, r2, w1, w2]
    route = jnp.zeros((tm, ROUTE_LANES), F32)
    for n, cval in enumerate(cols):
        route = jnp.where(lane == n, cval, route)
    route_ref[...] = route


def _router(x, g, wr, br, *, tm=512):
    m, k = x.shape
    return pl.pallas_call(
        _router_kernel,
        grid=(m // tm,),
        in_specs=[pl.BlockSpec((tm, k), lambda i: (i, 0)),
                  pl.BlockSpec((1, k), lambda i: (0, 0)),
                  pl.BlockSpec((k, ROUTE_LANES), lambda i: (0, 0)),
                  pl.BlockSpec((1, ROUTE_LANES), lambda i: (0, 0))],
        out_specs=[pl.BlockSpec((tm, k), lambda i: (i, 0)),
                   pl.BlockSpec((tm, ROUTE_LANES), lambda i: (i, 0)),
                   pl.BlockSpec((8, ROUTE_LANES), lambda i: (0, 0))],
        out_shape=[jax.ShapeDtypeStruct((m, k), F32),
                   jax.ShapeDtypeStruct((m, ROUTE_LANES), F32),
                   jax.ShapeDtypeStruct((8, ROUTE_LANES), F32)],
        scratch_shapes=[pltpu.VMEM((8, ROUTE_LANES), F32)],
        compiler_params=_cparams(1),
        name="router",
    )(x, g, wr, br)


def _gather_rows(idx_ref, n_rows, src_hbm, dst, sem):
    def body(r, carry):
        t = idx_ref[0, 0, r]
        pltpu.make_async_copy(src_hbm.at[pl.ds(t, 1), :], dst.at[pl.ds(r, 1), :], sem).start()
        return carry

    lax.fori_loop(0, n_rows, body, 0)


def _wait_rows(src_hbm, dst, sem):
    pltpu.make_async_copy(src_hbm.at[pl.ds(0, dst.shape[0]), :], dst, sem).wait()


def _expert_kernel(be_ref, nu_ref, src_cur, src_nxt, h_hbm, wg_ref, wu_ref, wd_ref, o_ref,
                   xbuf, sem, wg_b, wu_b, wd_b):
    b = pl.program_id(0)
    n_used = nu_ref[0]
    slot = b % 2

    @pl.when(b == 0)
    def _():
        _gather_rows(src_cur, EXPERT_BLOCK, h_hbm, xbuf.at[0], sem.at[0])

    @pl.when(b + 1 < n_used)
    def _():
        _gather_rows(src_nxt, EXPERT_BLOCK, h_hbm, xbuf.at[1 - slot], sem.at[1 - slot])

    @pl.when(b < n_used)
    def _():
        @pl.when((b == 0) | (be_ref[b] != be_ref[jnp.maximum(b - 1, 0)]))
        def _():
            wg_b[...] = wg_ref[0].astype(BF16)
            wu_b[...] = wu_ref[0].astype(BF16)
            wd_b[...] = wd_ref[0].astype(BF16)

        _wait_rows(h_hbm, xbuf.at[slot], sem.at[slot])
        x = xbuf[slot].astype(BF16)
        hid = (_silu(_dot(x, wg_b[...])) * _dot(x, wu_b[...])).astype(BF16)
        o_ref[...] = _dot(hid, wd_b[...])

    @pl.when(b >= n_used)
    def _():
        o_ref[...] = jnp.zeros(o_ref.shape, F32)


def _expert_ffn(block_expert, n_used, src, h, w_gate, w_up, w_down):
    nb = block_expert.shape[0]
    d = h.shape[1]
    nxt = lambda b, be, nu: (jnp.minimum(b + 1, nb - 1), 0, 0)
    grid_spec = pltpu.PrefetchScalarGridSpec(
        num_scalar_prefetch=2,
        grid=(nb,),
        in_specs=[pl.BlockSpec((1, 1, EXPERT_BLOCK), lambda b, be, nu: (b, 0, 0), memory_space=pltpu.SMEM),
                  pl.BlockSpec((1, 1, EXPERT_BLOCK), nxt, memory_space=pltpu.SMEM),
                  pl.BlockSpec(memory_space=pl.ANY),
                  pl.BlockSpec((1, d, D_EXPERT), lambda b, be, nu: (be[b], 0, 0)),
                  pl.BlockSpec((1, d, D_EXPERT), lambda b, be, nu: (be[b], 0, 0)),
                  pl.BlockSpec((1, D_EXPERT, d), lambda b, be, nu: (be[b], 0, 0))],
        out_specs=pl.BlockSpec((EXPERT_BLOCK, d), lambda b, be, nu: (b, 0)),
        scratch_shapes=[pltpu.VMEM((2, EXPERT_BLOCK, d), F32),
                        pltpu.SemaphoreType.DMA((2,)),
                        pltpu.VMEM((d, D_EXPERT), BF16),
                        pltpu.VMEM((d, D_EXPERT), BF16),
                        pltpu.VMEM((D_EXPERT, d), BF16)])
    return pl.pallas_call(
        _expert_kernel,
        grid_spec=grid_spec,
        out_shape=jax.ShapeDtypeStruct((nb * EXPERT_BLOCK, d), F32),
        compiler_params=_cparams(1, disable_bounds_checks=True),
        name="expert_ffn",
    )(block_expert, n_used, src, src, h, w_gate, w_up, w_down)


def _combine_kernel(pos_cur, pos_nxt, x_ref, route_ref, y_hbm, o_ref, ybuf, sem):
    i = pl.program_id(0)
    n = pl.num_programs(0)
    tm = x_ref.shape[0]
    slot = i % 2

    @pl.when(i == 0)
    def _():
        _gather_rows(pos_cur, TOP_K * tm, y_hbm, ybuf.at[0], sem.at[0])

    @pl.when(i + 1 < n)
    def _():
        _gather_rows(pos_nxt, TOP_K * tm, y_hbm, ybuf.at[1 - slot], sem.at[1 - slot])

    _wait_rows(y_hbm, ybuf.at[slot], sem.at[slot])
    r = route_ref[...]
    y = r[:, 4:5] * ybuf[slot, 0:tm, :] + r[:, 5:6] * ybuf[slot, tm:TOP_K * tm, :]
    o_ref[...] = x_ref[...] + y


def _combine(pos, x, route, yb, *, tm=256):
    m, d = x.shape
    nt = m // tm
    return pl.pallas_call(
        _combine_kernel,
        grid=(nt,),
        in_specs=[pl.BlockSpec((1, 1, TOP_K * tm), lambda i: (i, 0, 0), memory_space=pltpu.SMEM),
                  pl.BlockSpec((1, 1, TOP_K * tm), lambda i: (jnp.minimum(i + 1, nt - 1), 0, 0),
                               memory_space=pltpu.SMEM),
                  pl.BlockSpec((tm, d), lambda i: (i, 0)),
                  pl.BlockSpec((tm, ROUTE_LANES), lambda i: (i, 0)),
                  pl.BlockSpec(memory_space=pl.ANY)],
        out_specs=pl.BlockSpec((tm, d), lambda i: (i, 0)),
        out_shape=jax.ShapeDtypeStruct((m, d), F32),
        scratch_shapes=[pltpu.VMEM((2, TOP_K * tm, d), F32),
                        pltpu.SemaphoreType.DMA((2,))],
        compiler_params=_cparams(1, disable_bounds_checks=True),
        name="combine",
    )(pos, pos, x, route, yb)


def _hier_moe(x, g, wr, br, w_gate, w_up, w_down, *, combine_tm=256):
    n_tok = x.shape[0]
    h, route, cnt = _router(x, g, wr, br)
    counts = cnt[0, EXPERT_LANE0:EXPERT_LANE0 + N_EXPERTS].astype(I32)
    nblk = (counts + EXPERT_BLOCK - 1) // EXPERT_BLOCK
    bend = jnp.cumsum(nblk)
    bstart = bend - nblk
    nb = n_tok * TOP_K // EXPERT_BLOCK + N_EXPERTS
    expert = route[:, 0:TOP_K].astype(I32)
    rank = route[:, TOP_K:2 * TOP_K].astype(I32)
    pos = bstart[expert] * EXPERT_BLOCK + rank
    tok = jnp.broadcast_to(jnp.arange(n_tok, dtype=I32)[:, None], pos.shape)
    src = jnp.zeros((nb * EXPERT_BLOCK,), I32).at[pos.reshape(-1)].set(tok.reshape(-1))
    block_expert = jnp.minimum(jnp.searchsorted(bend, jnp.arange(nb, dtype=I32), side="right"),
                               N_EXPERTS - 1).astype(I32)
    yb = _expert_ffn(block_expert, bend[-1:].astype(I32), src.reshape(nb, 1, EXPERT_BLOCK),
                     h, w_gate, w_up, w_down)
    pos_tiles = pos.reshape(n_tok // combine_tm, combine_tm, TOP_K).transpose(0, 2, 1)
    pos_tiles = pos_tiles.reshape(n_tok // combine_tm, 1, TOP_K * combine_tm)
    return _combine(pos_tiles, x, route, yb, tm=combine_tm)


def _rope_tables(seqlen):
    inv_freq = ROPE_THETA ** (-jnp.arange(0, QK_ROPE, 2, dtype=F32) / QK_ROPE)
    ang = jnp.arange(seqlen, dtype=F32)[:, None] * inv_freq[None, :]
    cos, sin = jnp.cos(ang), jnp.sin(ang)
    zero = jnp.zeros((seqlen, LANES - QK_ROPE), F32)
    return (jnp.concatenate([cos, cos, zero], axis=-1), jnp.concatenate([-sin, sin, zero], axis=-1))


def _swap_halves(w):
    half = w.shape[-1] // 2
    return jnp.concatenate([w[..., half:], w[..., :half]], axis=-1)


def _router_weights(rg_w, rg_b, re_w, re_b):
    d = rg_w.shape[0]
    w = jnp.concatenate([rg_w, re_w.transpose(1, 0, 2).reshape(d, N_EXPERTS)], axis=1)
    b = jnp.concatenate([rg_b, re_b.reshape(N_EXPERTS)])
    pad = ROUTE_LANES - w.shape[1]
    return jnp.pad(w, ((0, 0), (0, pad))), jnp.pad(b, (0, pad)).reshape(1, ROUTE_LANES)


def _row(v, width=None):
    v = v.reshape(1, -1).astype(F32)
    if width is not None:
        v = jnp.pad(v, ((0, 0), (0, width - v.shape[1])))
    return v


def kernel(x, ssm_norm_g, ssm_in_w, ssm_conv_w, ssm_conv_b, ssm_dt_bias, ssm_A_log, ssm_D, ssm_gate_norm_g, ssm_out_w, kv_norm_g, kv_w_dkv, kv_latent_g, kv_w_uk, kv_w_uv, kv_w_kr, attn_norm_g, q_w_dq, q_latent_g, q_w_uq, attn_w_o, ffn_norm_g, router_group_w, router_group_b, router_expert_w, router_expert_b, expert_w_gate, expert_w_up, expert_w_down, final_norm_g):
    bsz, seqlen, d = x.shape
    n_tok = bsz * seqlen
    xs = x.reshape(n_tok, d)
    n_ssm = ssm_in_w.shape[0]
    depth = ffn_norm_g.shape[0]
    ck, sk = _rope_tables(seqlen)
    zx_cols = D_INNER + CONV_DIM
    k_all = v_all = None

    for layer in range(depth):
        if layer < n_ssm:
            i = layer
            w_in = ssm_in_w[i].astype(BF16)
            w_dt = jnp.pad(w_in[:, zx_cols:], ((0, 0), (0, LANES - SSM_HEADS)))
            zx, dt_raw = _in_proj(xs, _row(ssm_norm_g[i]), w_in, w_dt)
            yn = _ssd(zx, dt_raw, ssm_conv_w[i], _row(ssm_conv_b[i]), _row(ssm_dt_bias[i], LANES),
                      _row(ssm_A_log[i], LANES), _row(jnp.repeat(ssm_D[i], SSM_HEADDIM)),
                      _row(ssm_gate_norm_g[i]), bsz=bsz, seqlen=seqlen)
            xs = _mm_res(yn, ssm_out_w[i].astype(BF16), xs)
        else:
            j = layer - n_ssm
            if k_all is None:
                wkr2 = jnp.concatenate([kv_w_kr, _swap_halves(kv_w_kr)], axis=1).astype(BF16)
                k_all, v_all = _shared_kv(xs, _row(kv_norm_g), kv_w_dkv.astype(BF16), _row(kv_latent_g),
                                          kv_w_uk.astype(BF16), kv_w_uv.astype(BF16), wkr2, ck, sk,
                                          seqlen=seqlen)
            wq = q_w_uq[j].reshape(Q_LORA, MLA_HEADS, QK_NOPE + QK_ROPE)
            wn = wq[:, :, :QK_NOPE].reshape(Q_LORA, MLA_HEADS * QK_NOPE).astype(BF16)
            wr = wq[:, :, QK_NOPE:]
            lane_pad = ((0, 0), (0, 0), (0, LANES - QK_ROPE))
            wr_p = jnp.pad(wr, lane_pad).reshape(Q_LORA, MLA_HEADS * LANES).astype(BF16)
            ws_p = jnp.pad(_swap_halves(wr), lane_pad).reshape(Q_LORA, MLA_HEADS * LANES).astype(BF16)
            q_all = _q_side(xs, _row(attn_norm_g[j]), q_w_dq[j].astype(BF16), _row(q_latent_g[j]),
                            wn, wr_p, ws_p, ck, sk, seqlen=seqlen)
            o = _attention(q_all, k_all, v_all, bsz=bsz, seqlen=seqlen)
            xs = _mm_res(o, attn_w_o[j].astype(BF16), xs)
        wr_l, br_l = _router_weights(router_group_w[layer], router_group_b[layer],
                                     router_expert_w[layer], router_expert_b[layer])
        xs = _hier_moe(xs, _row(ffn_norm_g[layer]), wr_l, br_l, expert_w_gate[layer],
                       expert_w_up[layer], expert_w_down[layer])
    return _final_norm(xs, _row(final_norm_g)).reshape(bsz, seqlen, d)
```

```python
import functools
import math

import jax
import jax.numpy as jnp
from jax import lax
from jax.experimental import pallas as pl
from jax.experimental.pallas import tpu as pltpu

F32 = jnp.float32
BF16 = jnp.bfloat16
I32 = jnp.int32
HIGHEST = lax.Precision.HIGHEST

EPS = 1e-6
D_MODEL = 2048

D_INNER = 4096
SSM_HEADDIM = 64
SSM_HEADS = 64
SSM_GROUPS = 8
SSM_STATE = 128
CONV_WIDTH = 4
CHUNK = 128
CONV_DIM = D_INNER + 2 * SSM_GROUPS * SSM_STATE
GROUP_CH = D_INNER // SSM_GROUPS
CONV_HALO = 8

MLA_HEADS = 16
Q_LORA = 512
KV_LORA = 512
QK_NOPE = 128
QK_ROPE = 64
V_HEAD = 128
ROPE_THETA = 10000.0
QK_PAD = 256
ATT_TQ = 512
ATT_TK = 512
ATT_HEADS_PER_STEP = 2

N_GROUPS = 4
EXPERTS_PER_GROUP = 8
N_EXPERTS = 32
TOP_K = 2
D_EXPERT = 512
ROUTE_LANES = 128
EXPERT_LANE0 = N_GROUPS
EXPERT_BLOCK = 256

LANES = 128
VMEM_LIMIT = 56 * 1024 * 1024


def _cparams(n_axes, **kw):
    return pltpu.CompilerParams(dimension_semantics=("arbitrary",) * n_axes,
                                vmem_limit_bytes=VMEM_LIMIT, **kw)


def _rms(xf, g):
    return xf * lax.rsqrt(jnp.mean(xf * xf, axis=-1, keepdims=True) + EPS) * g


def _dot(a, b):
    return jnp.dot(a, b, preferred_element_type=F32)


def _silu(v):
    return v * jax.nn.sigmoid(v)


def _mm_res_kernel(a_ref, w_ref, r_ref, o_ref):
    o_ref[...] = r_ref[...] + _dot(a_ref[...], w_ref[...])


def _mm_res(a, w, res, *, tm=512, tn=512):
    m, k = a.shape
    n = w.shape[1]
    return pl.pallas_call(
        _mm_res_kernel,
        grid=(m // tm, n // tn),
        in_specs=[pl.BlockSpec((tm, k), lambda i, j: (i, 0)),
                  pl.BlockSpec((k, tn), lambda i, j: (0, j)),
                  pl.BlockSpec((tm, tn), lambda i, j: (i, j))],
        out_specs=pl.BlockSpec((tm, tn), lambda i, j: (i, j)),
        out_shape=jax.ShapeDtypeStruct((m, n), F32),
        compiler_params=_cparams(2),
        name="mm_res",
    )(a, w, res)


def _in_proj_kernel(x_ref, g_ref, w_ref, wdt_ref, o_ref, odt_ref, h_scr):
    @pl.when(pl.program_id(1) == 0)
    def _():
        h = _rms(x_ref[...], g_ref[...]).astype(BF16)
        h_scr[...] = h
        odt_ref[...] = _dot(h, wdt_ref[...])

    o_ref[...] = _dot(h_scr[...], w_ref[...]).astype(o_ref.dtype)


def _in_proj(x, g, w, wdt, *, tm=1024, tn=512):
    m, k = x.shape
    n = w.shape[1] // tn * tn
    tm = min(tm, m)
    return pl.pallas_call(
        _in_proj_kernel,
        grid=(m // tm, n // tn),
        in_specs=[pl.BlockSpec((tm, k), lambda i, j: (i, 0)),
                  pl.BlockSpec((1, k), lambda i, j: (0, 0)),
                  pl.BlockSpec((k, tn), lambda i, j: (0, j)),
                  pl.BlockSpec((k, LANES), lambda i, j: (0, 0))],
        out_specs=[pl.BlockSpec((tm, tn), lambda i, j: (i, j)),
                   pl.BlockSpec((tm, LANES), lambda i, j: (i, 0))],
        out_shape=[jax.ShapeDtypeStruct((m, n), BF16),
                   jax.ShapeDtypeStruct((m, LANES), F32)],
        scratch_shapes=[pltpu.VMEM((tm, k), BF16)],
        compiler_params=_cparams(2),
        name="in_proj",
    )(x, g, w, wdt)


def _final_norm_kernel(x_ref, g_ref, o_ref):
    o_ref[...] = _rms(x_ref[...], g_ref[...])


def _final_norm(x, g, *, tm=512):
    m, k = x.shape
    return pl.pallas_call(
        _final_norm_kernel,
        grid=(m // tm,),
        in_specs=[pl.BlockSpec((tm, k), lambda i: (i, 0)),
                  pl.BlockSpec((1, k), lambda i: (0, 0))],
        out_specs=pl.BlockSpec((tm, k), lambda i: (i, 0)),
        out_shape=jax.ShapeDtypeStruct((m, k), F32),
        compiler_params=_cparams(1),
        name="final_norm",
    )(x, g)


def _softplus(v):
    return jnp.maximum(v, 0.0) + jnp.log1p(jnp.exp(-jnp.abs(v)))


def _ssd_kernel(z_ref, x_ref, bc_ref, dt_ref, cw_ref, cb_ref, dtb_ref, alog_ref, dexp_ref, gn_ref,
                y_ref, ubuf, act, xw, st):
    L = CHUNK

    @pl.when(pl.program_id(1) == 0)
    def _():
        ubuf[0:CONV_HALO, :] = jnp.zeros((CONV_HALO, CONV_DIM), F32)
        st[...] = jnp.zeros(st.shape, F32)

    ubuf[CONV_HALO:CONV_HALO + L, 0:D_INNER] = x_ref[...].astype(F32)
    ubuf[CONV_HALO:CONV_HALO + L, D_INNER:CONV_DIM] = bc_ref[...].astype(F32)
    cblk = 512
    for j in range(CONV_DIM // cblk):
        cs_ = slice(j * cblk, (j + 1) * cblk)
        acc = cb_ref[:, cs_] + cw_ref[0:1, cs_] * ubuf[CONV_HALO - 3:CONV_HALO - 3 + L, cs_]
        for k in range(1, CONV_WIDTH):
            r0 = CONV_HALO - 3 + k
            acc = acc + cw_ref[k:k + 1, cs_] * ubuf[r0:r0 + L, cs_]
        act[:, cs_] = _silu(acc)
    ubuf[0:CONV_HALO, :] = ubuf[L:L + CONV_HALO, :]

    dtv = _softplus(dt_ref[...] + dtb_ref[...])
    a = dtv * (-jnp.exp(alog_ref[...]))
    row = lax.broadcasted_iota(I32, (L, L), 0)
    col = lax.broadcasted_iota(I32, (L, L), 1)
    causal = row >= col
    cs = jnp.dot(causal.astype(F32), a, precision=HIGHEST, preferred_element_type=F32)
    cs_t = cs.T
    dt_t = dtv.T
    ecs = jnp.exp(cs)
    wgt = dtv * jnp.exp(cs[L - 1:L, :] - cs)
    first_head = col < SSM_HEADDIM

    def pair(v, h0):
        return jnp.where(first_head, v[:, h0:h0 + 1], v[:, h0 + 1:h0 + 2])

    for g in range(SSM_GROUPS):
        b0 = D_INNER + g * SSM_STATE
        c0 = D_INNER + SSM_GROUPS * SSM_STATE + g * SSM_STATE
        bg = act[:, b0:b0 + SSM_STATE].astype(BF16)
        cg = act[:, c0:c0 + SSM_STATE].astype(BF16)
        cb = lax.dot_general(cg, bg, (((1,), (1,)), ((), ())), preferred_element_type=F32)
        st_g = st[g]
        y_off = _dot(cg, st_g.astype(BF16))
        for qq in range(GROUP_CH // LANES):
            lo = g * GROUP_CH + qq * LANES
            h0 = lo // SSM_HEADDIM
            xf = act[:, lo:lo + LANES]
            xb = xf.astype(BF16)

            def head_mat(h):
                dec = jnp.exp(jnp.where(causal, cs[:, h:h + 1] - cs_t[h:h + 1, :], -jnp.inf))
                return (cb * dec * dt_t[h:h + 1, :]).astype(BF16)

            y_diag = jnp.where(first_head, _dot(head_mat(h0), xb), _dot(head_mat(h0 + 1), xb))
            e_pair = pair(ecs, h0)
            y = y_diag + y_off[:, qq * LANES:(qq + 1) * LANES] * e_pair + xf * dexp_ref[:, lo:lo + LANES]
            zf = z_ref[:, lo:lo + LANES].astype(F32)
            act[:, lo:lo + LANES] = y * _silu(zf)
            xw[:, qq * LANES:(qq + 1) * LANES] = (xf * pair(wgt, h0)).astype(BF16)
            st[g, :, qq * LANES:(qq + 1) * LANES] = (
                st_g[:, qq * LANES:(qq + 1) * LANES] * e_pair[L - 1:L, :])
        upd = lax.dot_general(bg, xw[...], (((0,), (0,)), ((), ())), preferred_element_type=F32)
        st[g] = st[g] + upd
        gs = slice(g * GROUP_CH, (g + 1) * GROUP_CH)
        y_ref[:, gs] = _rms(act[:, gs], gn_ref[:, gs]).astype(y_ref.dtype)


def _ssd(zx, dt_raw, conv_w, conv_b, dt_bias, a_log, d_exp, gate_g, *, bsz, seqlen):
    nc = seqlen // CHUNK
    row = lambda b, c: b * nc + c
    full = lambda shape: pl.BlockSpec(shape, lambda b, c: (0,) * len(shape))
    x_blk = D_INNER // D_INNER
    bc_blk = (2 * D_INNER) // (CONV_DIM - D_INNER)
    return pl.pallas_call(
        _ssd_kernel,
        grid=(bsz, nc),
        in_specs=[pl.BlockSpec((CHUNK, D_INNER), lambda b, c: (row(b, c), 0)),
                  pl.BlockSpec((CHUNK, D_INNER), lambda b, c: (row(b, c), x_blk)),
                  pl.BlockSpec((CHUNK, CONV_DIM - D_INNER), lambda b, c: (row(b, c), bc_blk)),
                  pl.BlockSpec((CHUNK, LANES), lambda b, c: (row(b, c), 0)),
                  full((CONV_WIDTH, CONV_DIM)), full((1, CONV_DIM)), full((1, LANES)), full((1, LANES)),
                  full((1, D_INNER)), full((1, D_INNER))],
        out_specs=pl.BlockSpec((CHUNK, D_INNER), lambda b, c: (row(b, c), 0)),
        out_shape=jax.ShapeDtypeStruct((bsz * seqlen, D_INNER), BF16),
        scratch_shapes=[pltpu.VMEM((CHUNK + CONV_HALO, CONV_DIM), F32),
                        pltpu.VMEM((CHUNK, CONV_DIM), F32),
                        pltpu.VMEM((CHUNK, GROUP_CH), BF16),
                        pltpu.VMEM((SSM_GROUPS, SSM_STATE, GROUP_CH), F32)],
        compiler_params=_cparams(2),
        name="ssd",
    )(zx, zx, zx, dt_raw, conv_w, conv_b, dt_bias, a_log, d_exp, gate_g)


def _kv_kernel(x_ref, g_ref, wdkv_ref, lg_ref, wuk_ref, wuv_ref, wkr_ref, ck_ref, sk_ref,
               k_ref, v_ref):
    h = _rms(x_ref[...], g_ref[...]).astype(BF16)
    ckv = _rms(_dot(h, wdkv_ref[...]), lg_ref[...]).astype(BF16)
    kn = _dot(ckv, wuk_ref[...]).astype(BF16)
    v_ref[...] = _dot(ckv, wuv_ref[...]).astype(BF16)
    kk = _dot(h, wkr_ref[...])
    kr = (kk * ck_ref[...] + pltpu.roll(kk, QK_ROPE, 1) * sk_ref[...]).astype(BF16)
    for hd in range(MLA_HEADS):
        k_ref[:, hd * QK_PAD:hd * QK_PAD + QK_NOPE] = kn[:, hd * QK_NOPE:(hd + 1) * QK_NOPE]
        k_ref[:, hd * QK_PAD + QK_NOPE:(hd + 1) * QK_PAD] = kr


def _shared_kv(x, g, wdkv, lg, wuk, wuv, wkr2, ck, sk, *, seqlen, tm=512):
    m, k = x.shape
    nt = seqlen // tm
    full = lambda a: pl.BlockSpec(a.shape, lambda i: (0,) * a.ndim)
    tab = pl.BlockSpec((tm, LANES), lambda i: (i % nt, 0))
    return pl.pallas_call(
        _kv_kernel,
        grid=(m // tm,),
        in_specs=[pl.BlockSpec((tm, k), lambda i: (i, 0)), full(g), full(wdkv), full(lg), full(wuk),
                  full(wuv), full(wkr2), tab, tab],
        out_specs=[pl.BlockSpec((tm, MLA_HEADS * QK_PAD), lambda i: (i, 0)),
                   pl.BlockSpec((tm, MLA_HEADS * V_HEAD), lambda i: (i, 0))],
        out_shape=[jax.ShapeDtypeStruct((m, MLA_HEADS * QK_PAD), BF16),
                   jax.ShapeDtypeStruct((m, MLA_HEADS * V_HEAD), BF16)],
        compiler_params=_cparams(1),
        name="shared_kv",
    )(x, g, wdkv, lg, wuk, wuv, wkr2, ck, sk)


def _q_kernel(x_ref, g_ref, wdq_ref, lg_ref, wn_ref, wr_ref, ws_ref, ck_ref, sk_ref, q_ref):
    h = _rms(x_ref[...], g_ref[...]).astype(BF16)
    cq = _rms(_dot(h, wdq_ref[...]), lg_ref[...]).astype(BF16)
    qn = _dot(cq, wn_ref[...]).astype(BF16)
    qr = _dot(cq, wr_ref[...])
    qs = _dot(cq, ws_ref[...])
    ck = ck_ref[...]
    sk = sk_ref[...]
    for hd in range(MLA_HEADS):
        q_ref[:, hd * QK_PAD:hd * QK_PAD + QK_NOPE] = qn[:, hd * QK_NOPE:(hd + 1) * QK_NOPE]
        sl = slice(hd * LANES, (hd + 1) * LANES)
        q_ref[:, hd * QK_PAD + QK_NOPE:(hd + 1) * QK_PAD] = (qr[:, sl] * ck + qs[:, sl] * sk).astype(BF16)


def _q_side(x, g, wdq, lg, wn, wr, ws, ck, sk, *, seqlen, tm=512):
    m, k = x.shape
    nt = seqlen // tm
    full = lambda a: pl.BlockSpec(a.shape, lambda i: (0,) * a.ndim)
    tab = pl.BlockSpec((tm, LANES), lambda i: (i % nt, 0))
    return pl.pallas_call(
        _q_kernel,
        grid=(m // tm,),
        in_specs=[pl.BlockSpec((tm, k), lambda i: (i, 0)), full(g), full(wdq), full(lg), full(wn),
                  full(wr), full(ws), tab, tab],
        out_specs=pl.BlockSpec((tm, MLA_HEADS * QK_PAD), lambda i: (i, 0)),
        out_shape=jax.ShapeDtypeStruct((m, MLA_HEADS * QK_PAD), BF16),
        compiler_params=_cparams(1),
        name="q_side",
    )(x, g, wdq, lg, wn, wr, ws, ck, sk)


def _attn_kernel(q_ref, k_ref, v_ref, o_ref, m_scr, l_scr, acc_scr, s_scr):
    qi = pl.program_id(2)
    c1 = (QK_NOPE + QK_ROPE) ** -0.5 * math.log2(math.e)
    m_scr[...] = jnp.full(m_scr.shape, -jnp.inf, F32)
    l_scr[...] = jnp.zeros(l_scr.shape, F32)
    acc_scr[...] = jnp.zeros(acc_scr.shape, F32)

    def score(kc, slot):
        k0 = pl.multiple_of(kc * ATT_TK, ATT_TK)
        for hh in range(ATT_HEADS_PER_STEP):
            q = q_ref[:, hh * QK_PAD:(hh + 1) * QK_PAD]
            k = k_ref[pl.ds(k0, ATT_TK), hh * QK_PAD:(hh + 1) * QK_PAD]
            s_scr[slot, hh] = lax.dot_general(k, q, (((1,), (1,)), ((), ())), preferred_element_type=F32)

    def step(kc, slot, diagonal):
        k0 = pl.multiple_of(kc * ATT_TK, ATT_TK)
        for hh in range(ATT_HEADS_PER_STEP):
            v = v_ref[pl.ds(k0, ATT_TK), hh * V_HEAD:(hh + 1) * V_HEAD]
            s = s_scr[slot, hh] * c1
            if diagonal:
                key = lax.broadcasted_iota(I32, (ATT_TK, ATT_TQ), 0)
                qry = lax.broadcasted_iota(I32, (ATT_TK, ATT_TQ), 1)
                s = jnp.where(key <= qry, s, -jnp.inf)
            m_prev = m_scr[hh]
            m_new = jnp.maximum(m_prev, jnp.max(s, axis=0, keepdims=True))
            alpha = jnp.exp2(m_prev - m_new)
            p = jnp.exp2(s - m_new)
            l_scr[hh] = alpha * l_scr[hh] + jnp.sum(p, axis=0, keepdims=True)
            pv = lax.dot_general(v, p.astype(BF16), (((0,), (0,)), ((), ())), preferred_element_type=F32)
            acc_scr[hh] = alpha * acc_scr[hh] + pv
            m_scr[hh] = m_new

    def pair(j, carry):
        c = 2 * j
        score(c + 1, 1)
        step(c, 0, False)
        score(c + 2, 0)
        step(c + 1, 1, False)
        return carry

    score(0, 0)
    lax.fori_loop(0, qi // 2, pair, 0)

    @pl.when(qi % 2 == 0)
    def _():
        step(qi, 0, True)

    @pl.when(qi % 2 == 1)
    def _():
        score(qi, 1)
        step(qi - 1, 0, False)
        step(qi, 1, True)

    for hh in range(ATT_HEADS_PER_STEP):
        o_ref[:, hh * V_HEAD:(hh + 1) * V_HEAD] = (acc_scr[hh] / l_scr[hh]).T.astype(o_ref.dtype)


def _attention(q, k, v, *, bsz, seqlen):
    nq = seqlen // ATT_TQ
    hps = ATT_HEADS_PER_STEP
    return pl.pallas_call(
        _attn_kernel,
        grid=(bsz, MLA_HEADS // hps, nq),
        in_specs=[pl.BlockSpec((ATT_TQ, hps * QK_PAD), lambda b, h, i: (b * nq + i, h)),
                  pl.BlockSpec((seqlen, hps * QK_PAD), lambda b, h, i: (b, h)),
                  pl.BlockSpec((seqlen, hps * V_HEAD), lambda b, h, i: (b, h))],
        out_specs=pl.BlockSpec((ATT_TQ, hps * V_HEAD), lambda b, h, i: (b * nq + i, h)),
        out_shape=jax.ShapeDtypeStruct((bsz * seqlen, MLA_HEADS * V_HEAD), BF16),
        scratch_shapes=[pltpu.VMEM((hps, 1, ATT_TQ), F32), pltpu.VMEM((hps, 1, ATT_TQ), F32),
                        pltpu.VMEM((hps, V_HEAD, ATT_TQ), F32),
                        pltpu.VMEM((2, hps, ATT_TK, ATT_TQ), F32)],
        compiler_params=_cparams(3),
        name="attention",
    )(q, k, v)


def _router_kernel(x_ref, g_ref, wr_ref, br_ref, h_ref, route_ref, cnt_ref, carry):
    tm = x_ref.shape[0]

    @pl.when(pl.program_id(0) == 0)
    def _():
        carry[...] = jnp.zeros(carry.shape, F32)

    h = _rms(x_ref[...], g_ref[...])
    h_ref[...] = h
    logits = jnp.dot(h, wr_ref[...], precision=HIGHEST, preferred_element_type=F32) + br_ref[...]
    lane = lax.broadcasted_iota(I32, (tm, ROUTE_LANES), 1)
    neg = -jnp.inf

    def first_argmax(v, vmax):
        return jnp.min(jnp.where(v == vmax, lane, ROUTE_LANES), axis=-1, keepdims=True)

    gl = jnp.where(lane < N_GROUPS, logits, neg)
    gm = jnp.max(gl, axis=-1, keepdims=True)
    g_w = 1.0 / jnp.sum(jnp.exp(gl - gm), axis=-1, keepdims=True)
    gi = first_argmax(gl, gm)
    lo = EXPERT_LANE0 + gi * EXPERTS_PER_GROUP
    el = jnp.where((lane >= lo) & (lane < lo + EXPERTS_PER_GROUP), logits, neg)
    m1 = jnp.max(el, axis=-1, keepdims=True)
    es = jnp.sum(jnp.exp(el - m1), axis=-1, keepdims=True)
    i1 = first_argmax(el, m1)
    el2 = jnp.where(lane == i1, neg, el)
    m2 = jnp.max(el2, axis=-1, keepdims=True)
    i2 = first_argmax(el2, m2)
    p1 = 1.0 / es
    p2 = jnp.exp(m2 - m1) / es
    den = p1 + p2
    w1 = g_w * (p1 / den)
    w2 = g_w * (p2 / den)
    sel1 = lane == i1
    sel2 = lane == i2
    onehot = jnp.where(sel1 | sel2, 1.0, 0.0)
    r_i = lax.broadcasted_iota(I32, (tm, tm), 0)
    c_i = lax.broadcasted_iota(I32, (tm, tm), 1)
    before = jnp.where(r_i > c_i, 1.0, 0.0).astype(BF16)
    base = carry[0:1, :] + _dot(before, onehot.astype(BF16))
    r1 = jnp.sum(jnp.where(sel1, base, 0.0), axis=-1, keepdims=True)
    r2 = jnp.sum(jnp.where(sel2, base, 0.0), axis=-1, keepdims=True)
    carry[0:1, :] = carry[0:1, :] + jnp.sum(onehot, axis=0, keepdims=True)
    cnt_ref[...] = carry[...]
    cols = [(i1 - EXPERT_LANE0).astype(F32), (i2 - EXPERT_LANE0).astype(F32), r1, r2, w1, w2]
    route = jnp.zeros((tm, ROUTE_LANES), F32)
    for n, cval in enumerate(cols):
        route = jnp.where(lane == n, cval, route)
    route_ref[...] = route


def _router(x, g, wr, br, *, tm=512):
    m, k = x.shape
    return pl.pallas_call(
        _router_kernel,
        grid=(m // tm,),
        in_specs=[pl.BlockSpec((tm, k), lambda i: (i, 0)),
                  pl.BlockSpec((1, k), lambda i: (0, 0)),
                  pl.BlockSpec((k, ROUTE_LANES), lambda i: (0, 0)),
                  pl.BlockSpec((1, ROUTE_LANES), lambda i: (0, 0))],
        out_specs=[pl.BlockSpec((tm, k), lambda i: (i, 0)),
                   pl.BlockSpec((tm, ROUTE_LANES), lambda i: (i, 0)),
                   pl.BlockSpec((8, ROUTE_LANES), lambda i: (0, 0))],
        out_shape=[jax.ShapeDtypeStruct((m, k), F32),
                   jax.ShapeDtypeStruct((m, ROUTE_LANES), F32),
                   jax.ShapeDtypeStruct((8, ROUTE_LANES), F32)],
        scratch_shapes=[pltpu.VMEM((8, ROUTE_LANES), F32)],
        compiler_params=_cparams(1),
        name="router",
    )(x, g, wr, br)


def _gather_rows(idx_ref, n_rows, src_hbm, dst, sem):
    def body(r, carry):
        t = idx_ref[0, 0, r]
        pltpu.make_async_copy(src_hbm.at[pl.ds(t, 1), :], dst.at[pl.ds(r, 1), :], sem).start()
        return carry

    lax.fori_loop(0, n_rows, body, 0, unroll=8)


def _wait_rows(src_hbm, dst, sem):
    pltpu.make_async_copy(src_hbm.at[pl.ds(0, dst.shape[0]), :], dst, sem).wait()


def _expert_kernel(be_ref, nu_ref, src_cur, src_nxt, h_hbm, wg_ref, wu_ref, wd_ref, o_ref,
                   xbuf, sem, wg_b, wu_b, wd_b):
    b = pl.program_id(0)
    n_used = nu_ref[0]
    slot = b % 2

    @pl.when(b == 0)
    def _():
        _gather_rows(src_cur, EXPERT_BLOCK, h_hbm, xbuf.at[0], sem.at[0])

    @pl.when(b + 1 < n_used)
    def _():
        _gather_rows(src_nxt, EXPERT_BLOCK, h_hbm, xbuf.at[1 - slot], sem.at[1 - slot])

    @pl.when(b < n_used)
    def _():
        @pl.when((b == 0) | (be_ref[b] != be_ref[jnp.maximum(b - 1, 0)]))
        def _():
            wg_b[...] = wg_ref[0].astype(BF16)
            wu_b[...] = wu_ref[0].astype(BF16)
            wd_b[...] = wd_ref[0].astype(BF16)

        _wait_rows(h_hbm, xbuf.at[slot], sem.at[slot])
        x = xbuf[slot].astype(BF16)
        hid = (_silu(_dot(x, wg_b[...])) * _dot(x, wu_b[...])).astype(BF16)
        o_ref[...] = _dot(hid, wd_b[...])

    @pl.when(b >= n_used)
    def _():
        o_ref[...] = jnp.zeros(o_ref.shape, F32)


def _expert_ffn(block_expert, n_used, src, h, w_gate, w_up, w_down, *, layer):
    nb = block_expert.shape[0]
    d = h.shape[1]
    nxt = lambda b, be, nu: (jnp.minimum(b + 1, nb - 1), 0, 0)
    wsel = lambda b, be, nu: (layer * N_EXPERTS + be[b], 0, 0)
    grid_spec = pltpu.PrefetchScalarGridSpec(
        num_scalar_prefetch=2,
        grid=(nb,),
        in_specs=[pl.BlockSpec((1, 1, EXPERT_BLOCK), lambda b, be, nu: (b, 0, 0), memory_space=pltpu.SMEM),
                  pl.BlockSpec((1, 1, EXPERT_BLOCK), nxt, memory_space=pltpu.SMEM),
                  pl.BlockSpec(memory_space=pl.ANY),
                  pl.BlockSpec((1, d, D_EXPERT), wsel),
                  pl.BlockSpec((1, d, D_EXPERT), wsel),
                  pl.BlockSpec((1, D_EXPERT, d), wsel)],
        out_specs=pl.BlockSpec((EXPERT_BLOCK, d), lambda b, be, nu: (b, 0)),
        scratch_shapes=[pltpu.VMEM((2, EXPERT_BLOCK, d), F32),
                        pltpu.SemaphoreType.DMA((2,)),
                        pltpu.VMEM((d, D_EXPERT), BF16),
                        pltpu.VMEM((d, D_EXPERT), BF16),
                        pltpu.VMEM((D_EXPERT, d), BF16)])
    return pl.pallas_call(
        _expert_kernel,
        grid_spec=grid_spec,
        out_shape=jax.ShapeDtypeStruct((nb * EXPERT_BLOCK, d), F32),
        compiler_params=_cparams(1, disable_bounds_checks=True),
        name="expert_ffn",
    )(block_expert, n_used, src, src, h, w_gate, w_up, w_down)


def _combine_kernel(pos_cur, pos_nxt, x_ref, route_ref, y_hbm, o_ref, ybuf, sem):
    i = pl.program_id(0)
    n = pl.num_programs(0)
    tm = x_ref.shape[0]
    slot = i % 2

    @pl.when(i == 0)
    def _():
        _gather_rows(pos_cur, TOP_K * tm, y_hbm, ybuf.at[0], sem.at[0])

    @pl.when(i + 1 < n)
    def _():
        _gather_rows(pos_nxt, TOP_K * tm, y_hbm, ybuf.at[1 - slot], sem.at[1 - slot])

    _wait_rows(y_hbm, ybuf.at[slot], sem.at[slot])
    r = route_ref[...]
    y = r[:, 4:5] * ybuf[slot, 0:tm, :] + r[:, 5:6] * ybuf[slot, tm:TOP_K * tm, :]
    o_ref[...] = x_ref[...] + y


def _combine(pos, x, route, yb, *, tm=256):
    m, d = x.shape
    nt = m // tm
    return pl.pallas_call(
        _combine_kernel,
        grid=(nt,),
        in_specs=[pl.BlockSpec((1, 1, TOP_K * tm), lambda i: (i, 0, 0), memory_space=pltpu.SMEM),
                  pl.BlockSpec((1, 1, TOP_K * tm), lambda i: (jnp.minimum(i + 1, nt - 1), 0, 0),
                               memory_space=pltpu.SMEM),
                  pl.BlockSpec((tm, d), lambda i: (i, 0)),
                  pl.BlockSpec((tm, ROUTE_LANES), lambda i: (i, 0)),
                  pl.BlockSpec(memory_space=pl.ANY)],
        out_specs=pl.BlockSpec((tm, d), lambda i: (i, 0)),
        out_shape=jax.ShapeDtypeStruct((m, d), F32),
        scratch_shapes=[pltpu.VMEM((2, TOP_K * tm, d), F32),
                        pltpu.SemaphoreType.DMA((2,))],
        compiler_params=_cparams(1, disable_bounds_checks=True),
        name="combine",
    )(pos, pos, x, route, yb)


def _hier_moe(x, g, wr, br, w_gate, w_up, w_down, *, layer, combine_tm=256):
    n_tok = x.shape[0]
    h, route, cnt = _router(x, g, wr, br)
    counts = cnt[0, EXPERT_LANE0:EXPERT_LANE0 + N_EXPERTS].astype(I32)
    nblk = (counts + EXPERT_BLOCK - 1) // EXPERT_BLOCK
    bend = jnp.cumsum(nblk)
    bstart = bend - nblk
    nb = n_tok * TOP_K // EXPERT_BLOCK + N_EXPERTS
    expert = route[:, 0:TOP_K].astype(I32)
    rank = route[:, TOP_K:2 * TOP_K].astype(I32)
    pos = bstart[expert] * EXPERT_BLOCK + rank
    tok = jnp.broadcast_to(jnp.arange(n_tok, dtype=I32)[:, None], pos.shape)
    src = jnp.zeros((nb * EXPERT_BLOCK,), I32).at[pos.reshape(-1)].set(tok.reshape(-1))
    block_expert = jnp.sum((jnp.arange(nb, dtype=I32)[:, None] >= bend[None, :]).astype(I32), axis=1)
    block_expert = jnp.minimum(block_expert, N_EXPERTS - 1)
    yb = _expert_ffn(block_expert, bend[-1:].astype(I32), src.reshape(nb, 1, EXPERT_BLOCK),
                     h, w_gate, w_up, w_down, layer=layer)
    pos_tiles = pos.reshape(n_tok // combine_tm, combine_tm, TOP_K).transpose(0, 2, 1)
    pos_tiles = pos_tiles.reshape(n_tok // combine_tm, 1, TOP_K * combine_tm)
    return _combine(pos_tiles, x, route, yb, tm=combine_tm)


def _rope_tables(seqlen):
    inv_freq = ROPE_THETA ** (-jnp.arange(0, QK_ROPE, 2, dtype=F32) / QK_ROPE)
    ang = jnp.arange(seqlen, dtype=F32)[:, None] * inv_freq[None, :]
    cos, sin = jnp.cos(ang), jnp.sin(ang)
    zero = jnp.zeros((seqlen, LANES - QK_ROPE), F32)
    return (jnp.concatenate([cos, cos, zero], axis=-1), jnp.concatenate([-sin, sin, zero], axis=-1))


def _swap_halves(w):
    half = w.shape[-1] // 2
    return jnp.concatenate([w[..., half:], w[..., :half]], axis=-1)


def _router_weights(rg_w, rg_b, re_w, re_b):
    d = rg_w.shape[0]
    w = jnp.concatenate([rg_w, re_w.transpose(1, 0, 2).reshape(d, N_EXPERTS)], axis=1)
    b = jnp.concatenate([rg_b, re_b.reshape(N_EXPERTS)])
    pad = ROUTE_LANES - w.shape[1]
    return jnp.pad(w, ((0, 0), (0, pad))), jnp.pad(b, (0, pad)).reshape(1, ROUTE_LANES)


def _row(v, width=None):
    v = v.reshape(1, -1).astype(F32)
    if width is not None:
        v = jnp.pad(v, ((0, 0), (0, width - v.shape[1])))
    return v


def kernel(x, ssm_norm_g, ssm_in_w, ssm_conv_w, ssm_conv_b, ssm_dt_bias, ssm_A_log, ssm_D, ssm_gate_norm_g, ssm_out_w, kv_norm_g, kv_w_dkv, kv_latent_g, kv_w_uk, kv_w_uv, kv_w_kr, attn_norm_g, q_w_dq, q_latent_g, q_w_uq, attn_w_o, ffn_norm_g, router_group_w, router_group_b, router_expert_w, router_expert_b, expert_w_gate, expert_w_up, expert_w_down, final_norm_g):
    bsz, seqlen, d = x.shape
    n_tok = bsz * seqlen
    xs = x.reshape(n_tok, d)
    n_ssm = ssm_in_w.shape[0]
    depth = ffn_norm_g.shape[0]
    ck, sk = _rope_tables(seqlen)
    zx_cols = D_INNER + CONV_DIM
    k_all = v_all = None
    w_gate_all = expert_w_gate.reshape(depth * N_EXPERTS, d, D_EXPERT)
    w_up_all = expert_w_up.reshape(depth * N_EXPERTS, d, D_EXPERT)
    w_down_all = expert_w_down.reshape(depth * N_EXPERTS, D_EXPERT, d)

    for layer in range(depth):
        if layer < n_ssm:
            i = layer
            w_in = ssm_in_w[i].astype(BF16)
            w_dt = jnp.pad(w_in[:, zx_cols:], ((0, 0), (0, LANES - SSM_HEADS)))
            zx, dt_raw = _in_proj(xs, _row(ssm_norm_g[i]), w_in, w_dt)
            yn = _ssd(zx, dt_raw, ssm_conv_w[i], _row(ssm_conv_b[i]), _row(ssm_dt_bias[i], LANES),
                      _row(ssm_A_log[i], LANES), _row(jnp.repeat(ssm_D[i], SSM_HEADDIM)),
                      _row(ssm_gate_norm_g[i]), bsz=bsz, seqlen=seqlen)
            xs = _mm_res(yn, ssm_out_w[i].astype(BF16), xs)
        else:
            j = layer - n_ssm
            if k_all is None:
                wkr2 = jnp.concatenate([kv_w_kr, _swap_halves(kv_w_kr)], axis=1).astype(BF16)
                k_all, v_all = _shared_kv(xs, _row(kv_norm_g), kv_w_dkv.astype(BF16), _row(kv_latent_g),
                                          kv_w_uk.astype(BF16), kv_w_uv.astype(BF16), wkr2, ck, sk,
                                          seqlen=seqlen)
            wq = q_w_uq[j].reshape(Q_LORA, MLA_HEADS, QK_NOPE + QK_ROPE)
            wn = wq[:, :, :QK_NOPE].reshape(Q_LORA, MLA_HEADS * QK_NOPE).astype(BF16)
            wr = wq[:, :, QK_NOPE:]
            lane_pad = ((0, 0), (0, 0), (0, LANES - QK_ROPE))
            wr_p = jnp.pad(wr, lane_pad).reshape(Q_LORA, MLA_HEADS * LANES).astype(BF16)
            ws_p = jnp.pad(_swap_halves(wr), lane_pad).reshape(Q_LORA, MLA_HEADS * LANES).astype(BF16)
            q_all = _q_side(xs, _row(attn_norm_g[j]), q_w_dq[j].astype(BF16), _row(q_latent_g[j]),
                            wn, wr_p, ws_p, ck, sk, seqlen=seqlen)
            o = _attention(q_all, k_all, v_all, bsz=bsz, seqlen=seqlen)
            xs = _mm_res(o, attn_w_o[j].astype(BF16), xs)
        wr_l, br_l = _router_weights(router_group_w[layer], router_group_b[layer],
                                     router_expert_w[layer], router_expert_b[layer])
        xs = _hier_moe(xs, _row(ffn_norm_g[layer]), wr_l, br_l, w_gate_all, w_up_all, w_down_all, layer=layer)
    return _final_norm(xs, _row(final_norm_g)).reshape(bsz, seqlen, d)
```

```python
import functools
import math

import jax
import jax.numpy as jnp
from jax import lax
from jax.experimental import pallas as pl
from jax.experimental.pallas import tpu as pltpu

F32 = jnp.float32
BF16 = jnp.bfloat16
I32 = jnp.int32
HIGHEST = lax.Precision.HIGHEST

EPS = 1e-6
D_MODEL = 2048

D_INNER = 4096
SSM_HEADDIM = 64
SSM_HEADS = 64
SSM_GROUPS = 8
SSM_STATE = 128
CONV_WIDTH = 4
CHUNK = 128
CONV_DIM = D_INNER + 2 * SSM_GROUPS * SSM_STATE
GROUP_CH = D_INNER // SSM_GROUPS
CONV_HALO = 8

MLA_HEADS = 16
Q_LORA = 512
KV_LORA = 512
QK_NOPE = 128
QK_ROPE = 64
V_HEAD = 128
ROPE_THETA = 10000.0
QK_PAD = 256
ATT_TQ = 512
ATT_TK = 512
ATT_HEADS_PER_STEP = 2

N_GROUPS = 4
EXPERTS_PER_GROUP = 8
N_EXPERTS = 32
TOP_K = 2
D_EXPERT = 512
ROUTE_LANES = 128
EXPERT_LANE0 = N_GROUPS
EXPERT_BLOCK = 256
TOKEN_CHUNKS = D_MODEL // 128
TOKEN_PITCH = TOKEN_CHUNKS + 1

LANES = 128
VMEM_LIMIT = 56 * 1024 * 1024


def _cparams(n_axes, **kw):
    return pltpu.CompilerParams(dimension_semantics=("arbitrary",) * n_axes,
                                vmem_limit_bytes=VMEM_LIMIT, **kw)


def _rms(xf, g):
    return xf * lax.rsqrt(jnp.mean(xf * xf, axis=-1, keepdims=True) + EPS) * g


def _dot(a, b):
    return jnp.dot(a, b, preferred_element_type=F32)


def _silu(v):
    return v * jax.nn.sigmoid(v)


def _mm_res_kernel(a_ref, w_ref, r_ref, o_ref):
    o_ref[...] = r_ref[...] + _dot(a_ref[...], w_ref[...])


def _mm_res(a, w, res, *, tm=1024):
    m, k = a.shape
    n = w.shape[1]
    tm = min(tm, m)
    tn = min(n, 1024 if k <= 2048 else 512)
    return pl.pallas_call(
        _mm_res_kernel,
        grid=(m // tm, n // tn),
        in_specs=[pl.BlockSpec((tm, k), lambda i, j: (i, 0)),
                  pl.BlockSpec((k, tn), lambda i, j: (0, j)),
                  pl.BlockSpec((tm, tn), lambda i, j: (i, j))],
        out_specs=pl.BlockSpec((tm, tn), lambda i, j: (i, j)),
        out_shape=jax.ShapeDtypeStruct((m, n), F32),
        compiler_params=_cparams(2),
        name="mm_res",
    )(a, w, res)


def _in_proj_kernel(x_ref, g_ref, w_ref, wdt_ref, o_ref, odt_ref, h_scr):
    @pl.when(pl.program_id(1) == 0)
    def _():
        h = _rms(x_ref[...], g_ref[...]).astype(BF16)
        h_scr[...] = h
        odt_ref[...] = _dot(h, wdt_ref[...])

    o_ref[...] = _dot(h_scr[...], w_ref[...]).astype(o_ref.dtype)


def _in_proj(x, g, w, wdt, *, tm=1024, tn=512):
    m, k = x.shape
    n = w.shape[1] // tn * tn
    tm = min(tm, m)
    return pl.pallas_call(
        _in_proj_kernel,
        grid=(m // tm, n // tn),
        in_specs=[pl.BlockSpec((tm, k), lambda i, j: (i, 0)),
                  pl.BlockSpec((1, k), lambda i, j: (0, 0)),
                  pl.BlockSpec((k, tn), lambda i, j: (0, j)),
                  pl.BlockSpec((k, LANES), lambda i, j: (0, 0))],
        out_specs=[pl.BlockSpec((tm, tn), lambda i, j: (i, j)),
                   pl.BlockSpec((tm, LANES), lambda i, j: (i, 0))],
        out_shape=[jax.ShapeDtypeStruct((m, n), BF16),
                   jax.ShapeDtypeStruct((m, LANES), F32)],
        scratch_shapes=[pltpu.VMEM((tm, k), BF16)],
        compiler_params=_cparams(2),
        name="in_proj",
    )(x, g, w, wdt)


def _final_norm_kernel(x_ref, g_ref, o_ref):
    o_ref[...] = _rms(x_ref[...], g_ref[...])


def _final_norm(x, g, *, tm=512):
    m, k = x.shape
    return pl.pallas_call(
        _final_norm_kernel,
        grid=(m // tm,),
        in_specs=[pl.BlockSpec((tm, k), lambda i: (i, 0)),
                  pl.BlockSpec((1, k), lambda i: (0, 0))],
        out_specs=pl.BlockSpec((tm, k), lambda i: (i, 0)),
        out_shape=jax.ShapeDtypeStruct((m, k), F32),
        compiler_params=_cparams(1),
        name="final_norm",
    )(x, g)


def _softplus(v):
    return jnp.maximum(v, 0.0) + jnp.log1p(jnp.exp(-jnp.abs(v)))


def _ssd_kernel(z_ref, x_ref, bc_ref, dt_ref, cw_ref, cb_ref, dtb_ref, alog_ref, dexp_ref, gn_ref,
                y_ref, ubuf, act, xw, st):
    L = CHUNK

    @pl.when(pl.program_id(1) == 0)
    def _():
        ubuf[0:CONV_HALO, :] = jnp.zeros((CONV_HALO, CONV_DIM), F32)
        st[...] = jnp.zeros(st.shape, F32)

    ubuf[CONV_HALO:CONV_HALO + L, 0:D_INNER] = x_ref[...].astype(F32)
    ubuf[CONV_HALO:CONV_HALO + L, D_INNER:CONV_DIM] = bc_ref[...].astype(F32)
    cblk = 512
    for j in range(CONV_DIM // cblk):
        cs_ = slice(j * cblk, (j + 1) * cblk)
        acc = cb_ref[:, cs_] + cw_ref[0:1, cs_] * ubuf[CONV_HALO - 3:CONV_HALO - 3 + L, cs_]
        for k in range(1, CONV_WIDTH):
            r0 = CONV_HALO - 3 + k
            acc = acc + cw_ref[k:k + 1, cs_] * ubuf[r0:r0 + L, cs_]
        act[:, cs_] = _silu(acc)
    ubuf[0:CONV_HALO, :] = ubuf[L:L + CONV_HALO, :]

    dtv = _softplus(dt_ref[...] + dtb_ref[...])
    a = dtv * (-jnp.exp(alog_ref[...]))
    row = lax.broadcasted_iota(I32, (L, L), 0)
    col = lax.broadcasted_iota(I32, (L, L), 1)
    causal = row >= col
    cs = jnp.dot(causal.astype(F32), a, precision=HIGHEST, preferred_element_type=F32)
    cs_t = cs.T
    dt_t = dtv.T
    ecs = jnp.exp(cs)
    wgt = dtv * jnp.exp(cs[L - 1:L, :] - cs)
    first_head = col < SSM_HEADDIM

    def pair(v, h0):
        return jnp.where(first_head, v[:, h0:h0 + 1], v[:, h0 + 1:h0 + 2])

    for g in range(SSM_GROUPS):
        b0 = D_INNER + g * SSM_STATE
        c0 = D_INNER + SSM_GROUPS * SSM_STATE + g * SSM_STATE
        bg = act[:, b0:b0 + SSM_STATE].astype(BF16)
        cg = act[:, c0:c0 + SSM_STATE].astype(BF16)
        cb = lax.dot_general(cg, bg, (((1,), (1,)), ((), ())), preferred_element_type=F32)
        st_g = st[g]
        y_off = _dot(cg, st_g.astype(BF16))
        for qq in range(GROUP_CH // LANES):
            lo = g * GROUP_CH + qq * LANES
            h0 = lo // SSM_HEADDIM
            xf = act[:, lo:lo + LANES]
            xb = xf.astype(BF16)

            def head_mat(h):
                dec = jnp.exp(jnp.where(causal, cs[:, h:h + 1] - cs_t[h:h + 1, :], -jnp.inf))
                return (cb * dec * dt_t[h:h + 1, :]).astype(BF16)

            y_diag = jnp.where(first_head, _dot(head_mat(h0), xb), _dot(head_mat(h0 + 1), xb))
            e_pair = pair(ecs, h0)
            y = y_diag + y_off[:, qq * LANES:(qq + 1) * LANES] * e_pair + xf * dexp_ref[:, lo:lo + LANES]
            zf = z_ref[:, lo:lo + LANES].astype(F32)
            act[:, lo:lo + LANES] = y * _silu(zf)
            xw[:, qq * LANES:(qq + 1) * LANES] = (xf * pair(wgt, h0)).astype(BF16)
            st[g, :, qq * LANES:(qq + 1) * LANES] = (
                st_g[:, qq * LANES:(qq + 1) * LANES] * e_pair[L - 1:L, :])
        upd = lax.dot_general(bg, xw[...], (((0,), (0,)), ((), ())), preferred_element_type=F32)
        st[g] = st[g] + upd
        gs = slice(g * GROUP_CH, (g + 1) * GROUP_CH)
        y_ref[:, gs] = _rms(act[:, gs], gn_ref[:, gs]).astype(y_ref.dtype)


def _ssd(zx, dt_raw, conv_w, conv_b, dt_bias, a_log, d_exp, gate_g, *, bsz, seqlen):
    nc = seqlen // CHUNK
    row = lambda b, c: b * nc + c
    full = lambda shape: pl.BlockSpec(shape, lambda b, c: (0,) * len(shape))
    x_blk = D_INNER // D_INNER
    bc_blk = (2 * D_INNER) // (CONV_DIM - D_INNER)
    return pl.pallas_call(
        _ssd_kernel,
        grid=(bsz, nc),
        in_specs=[pl.BlockSpec((CHUNK, D_INNER), lambda b, c: (row(b, c), 0)),
                  pl.BlockSpec((CHUNK, D_INNER), lambda b, c: (row(b, c), x_blk)),
                  pl.BlockSpec((CHUNK, CONV_DIM - D_INNER), lambda b, c: (row(b, c), bc_blk)),
                  pl.BlockSpec((CHUNK, LANES), lambda b, c: (row(b, c), 0)),
                  full((CONV_WIDTH, CONV_DIM)), full((1, CONV_DIM)), full((1, LANES)), full((1, LANES)),
                  full((1, D_INNER)), full((1, D_INNER))],
        out_specs=pl.BlockSpec((CHUNK, D_INNER), lambda b, c: (row(b, c), 0)),
        out_shape=jax.ShapeDtypeStruct((bsz * seqlen, D_INNER), BF16),
        scratch_shapes=[pltpu.VMEM((CHUNK + CONV_HALO, CONV_DIM), F32),
                        pltpu.VMEM((CHUNK, CONV_DIM), F32),
                        pltpu.VMEM((CHUNK, GROUP_CH), BF16),
                        pltpu.VMEM((SSM_GROUPS, SSM_STATE, GROUP_CH), F32)],
        compiler_params=_cparams(2),
        name="ssd",
    )(zx, zx, zx, dt_raw, conv_w, conv_b, dt_bias, a_log, d_exp, gate_g)


def _kv_kernel(x_ref, g_ref, wdkv_ref, lg_ref, wuk_ref, wuv_ref, wkr_ref, ck_ref, sk_ref,
               k_ref, v_ref):
    h = _rms(x_ref[...], g_ref[...]).astype(BF16)
    ckv = _rms(_dot(h, wdkv_ref[...]), lg_ref[...]).astype(BF16)
    kn = _dot(ckv, wuk_ref[...]).astype(BF16)
    v_ref[...] = _dot(ckv, wuv_ref[...]).astype(BF16)
    kk = _dot(h, wkr_ref[...])
    kr = (kk * ck_ref[...] + pltpu.roll(kk, QK_ROPE, 1) * sk_ref[...]).astype(BF16)
    for hd in range(MLA_HEADS):
        k_ref[:, hd * QK_PAD:hd * QK_PAD + QK_NOPE] = kn[:, hd * QK_NOPE:(hd + 1) * QK_NOPE]
        k_ref[:, hd * QK_PAD + QK_NOPE:(hd + 1) * QK_PAD] = kr


def _shared_kv(x, g, wdkv, lg, wuk, wuv, wkr2, ck, sk, *, seqlen, tm=512):
    m, k = x.shape
    nt = seqlen // tm
    full = lambda a: pl.BlockSpec(a.shape, lambda i: (0,) * a.ndim)
    tab = pl.BlockSpec((tm, LANES), lambda i: (i % nt, 0))
    return pl.pallas_call(
        _kv_kernel,
        grid=(m // tm,),
        in_specs=[pl.BlockSpec((tm, k), lambda i: (i, 0)), full(g), full(wdkv), full(lg), full(wuk),
                  full(wuv), full(wkr2), tab, tab],
        out_specs=[pl.BlockSpec((tm, MLA_HEADS * QK_PAD), lambda i: (i, 0)),
                   pl.BlockSpec((tm, MLA_HEADS * V_HEAD), lambda i: (i, 0))],
        out_shape=[jax.ShapeDtypeStruct((m, MLA_HEADS * QK_PAD), BF16),
                   jax.ShapeDtypeStruct((m, MLA_HEADS * V_HEAD), BF16)],
        compiler_params=_cparams(1),
        name="shared_kv",
    )(x, g, wdkv, lg, wuk, wuv, wkr2, ck, sk)


def _q_kernel(x_ref, g_ref, wdq_ref, lg_ref, wn_ref, wr_ref, ws_ref, ck_ref, sk_ref, q_ref):
    h = _rms(x_ref[...], g_ref[...]).astype(BF16)
    cq = _rms(_dot(h, wdq_ref[...]), lg_ref[...]).astype(BF16)
    qn = _dot(cq, wn_ref[...]).astype(BF16)
    qr = _dot(cq, wr_ref[...])
    qs = _dot(cq, ws_ref[...])
    ck = ck_ref[...]
    sk = sk_ref[...]
    for hd in range(MLA_HEADS):
        q_ref[:, hd * QK_PAD:hd * QK_PAD + QK_NOPE] = qn[:, hd * QK_NOPE:(hd + 1) * QK_NOPE]
        sl = slice(hd * LANES, (hd + 1) * LANES)
        q_ref[:, hd * QK_PAD + QK_NOPE:(hd + 1) * QK_PAD] = (qr[:, sl] * ck + qs[:, sl] * sk).astype(BF16)


def _q_side(x, g, wdq, lg, wn, wr, ws, ck, sk, *, seqlen, tm=512):
    m, k = x.shape
    nt = seqlen // tm
    full = lambda a: pl.BlockSpec(a.shape, lambda i: (0,) * a.ndim)
    tab = pl.BlockSpec((tm, LANES), lambda i: (i % nt, 0))
    return pl.pallas_call(
        _q_kernel,
        grid=(m // tm,),
        in_specs=[pl.BlockSpec((tm, k), lambda i: (i, 0)), full(g), full(wdq), full(lg), full(wn),
                  full(wr), full(ws), tab, tab],
        out_specs=pl.BlockSpec((tm, MLA_HEADS * QK_PAD), lambda i: (i, 0)),
        out_shape=jax.ShapeDtypeStruct((m, MLA_HEADS * QK_PAD), BF16),
        compiler_params=_cparams(1),
        name="q_side",
    )(x, g, wdq, lg, wn, wr, ws, ck, sk)


def _attn_kernel(q_ref, k_ref, v_ref, o_ref, m_scr, l_scr, acc_scr, s_scr):
    qi = pl.program_id(2)
    c1 = (QK_NOPE + QK_ROPE) ** -0.5 * math.log2(math.e)
    m_scr[...] = jnp.full(m_scr.shape, -jnp.inf, F32)
    l_scr[...] = jnp.zeros(l_scr.shape, F32)
    acc_scr[...] = jnp.zeros(acc_scr.shape, F32)

    def score(kc, slot):
        k0 = pl.multiple_of(kc * ATT_TK, ATT_TK)
        for hh in range(ATT_HEADS_PER_STEP):
            q = q_ref[:, hh * QK_PAD:(hh + 1) * QK_PAD]
            k = k_ref[pl.ds(k0, ATT_TK), hh * QK_PAD:(hh + 1) * QK_PAD]
            s_scr[slot, hh] = lax.dot_general(k, q, (((1,), (1,)), ((), ())), preferred_element_type=F32)

    def step(kc, slot, diagonal):
        k0 = pl.multiple_of(kc * ATT_TK, ATT_TK)
        for hh in range(ATT_HEADS_PER_STEP):
            v = v_ref[pl.ds(k0, ATT_TK), hh * V_HEAD:(hh + 1) * V_HEAD]
            s = s_scr[slot, hh] * c1
            if diagonal:
                key = lax.broadcasted_iota(I32, (ATT_TK, ATT_TQ), 0)
                qry = lax.broadcasted_iota(I32, (ATT_TK, ATT_TQ), 1)
                s = jnp.where(key <= qry, s, -jnp.inf)
            m_prev = m_scr[hh]
            m_new = jnp.maximum(m_prev, jnp.max(s, axis=0, keepdims=True))
            alpha = jnp.exp2(m_prev - m_new)
            p = jnp.exp2(s - m_new)
            l_scr[hh] = alpha * l_scr[hh] + jnp.sum(p, axis=0, keepdims=True)
            pv = lax.dot_general(v, p.astype(BF16), (((0,), (0,)), ((), ())), preferred_element_type=F32)
            acc_scr[hh] = alpha * acc_scr[hh] + pv
            m_scr[hh] = m_new

    def pair(j, carry):
        c = 2 * j
        score(c + 1, 1)
        step(c, 0, False)
        score(c + 2, 0)
        step(c + 1, 1, False)
        return carry

    score(0, 0)
    lax.fori_loop(0, qi // 2, pair, 0)

    @pl.when(qi % 2 == 0)
    def _():
        step(qi, 0, True)

    @pl.when(qi % 2 == 1)
    def _():
        score(qi, 1)
        step(qi - 1, 0, False)
        step(qi, 1, True)

    for hh in range(ATT_HEADS_PER_STEP):
        o_ref[:, hh * V_HEAD:(hh + 1) * V_HEAD] = (acc_scr[hh] / l_scr[hh]).T.astype(o_ref.dtype)


def _attention(q, k, v, *, bsz, seqlen):
    nq = seqlen // ATT_TQ
    hps = ATT_HEADS_PER_STEP
    return pl.pallas_call(
        _attn_kernel,
        grid=(bsz, MLA_HEADS // hps, nq),
        in_specs=[pl.BlockSpec((ATT_TQ, hps * QK_PAD), lambda b, h, i: (b * nq + i, h)),
                  pl.BlockSpec((seqlen, hps * QK_PAD), lambda b, h, i: (b, h)),
                  pl.BlockSpec((seqlen, hps * V_HEAD), lambda b, h, i: (b, h))],
        out_specs=pl.BlockSpec((ATT_TQ, hps * V_HEAD), lambda b, h, i: (b * nq + i, h)),
        out_shape=jax.ShapeDtypeStruct((bsz * seqlen, MLA_HEADS * V_HEAD), BF16),
        scratch_shapes=[pltpu.VMEM((hps, 1, ATT_TQ), F32), pltpu.VMEM((hps, 1, ATT_TQ), F32),
                        pltpu.VMEM((hps, V_HEAD, ATT_TQ), F32),
                        pltpu.VMEM((2, hps, ATT_TK, ATT_TQ), F32)],
        compiler_params=_cparams(3),
        name="attention",
    )(q, k, v)


def _router_kernel(x_ref, g_ref, wr_ref, br_ref, h_ref, route_ref, cnt_ref, carry):
    tm = x_ref.shape[0]

    @pl.when(pl.program_id(0) == 0)
    def _():
        carry[...] = jnp.zeros(carry.shape, F32)

    h = _rms(x_ref[...], g_ref[...])
    _store_token_rows(h_ref, h)
    h_hi = h.astype(BF16)
    h_lo = (h - h_hi.astype(F32)).astype(BF16)
    hw = _dot(h_hi, wr_ref[...])
    logits = (hw[:, :ROUTE_LANES] + hw[:, ROUTE_LANES:] + _dot(h_lo, wr_ref[:, :ROUTE_LANES])) + br_ref[...]
    lane = lax.broadcasted_iota(I32, (tm, ROUTE_LANES), 1)
    neg = -jnp.inf

    def first_argmax(v, vmax):
        return jnp.min(jnp.where(v == vmax, lane, ROUTE_LANES), axis=-1, keepdims=True)

    gl = jnp.where(lane < N_GROUPS, logits, neg)
    gm = jnp.max(gl, axis=-1, keepdims=True)
    g_w = 1.0 / jnp.sum(jnp.exp(gl - gm), axis=-1, keepdims=True)
    gi = first_argmax(gl, gm)
    lo = EXPERT_LANE0 + gi * EXPERTS_PER_GROUP
    el = jnp.where((lane >= lo) & (lane < lo + EXPERTS_PER_GROUP), logits, neg)
    m1 = jnp.max(el, axis=-1, keepdims=True)
    es = jnp.sum(jnp.exp(el - m1), axis=-1, keepdims=True)
    i1 = first_argmax(el, m1)
    el2 = jnp.where(lane == i1, neg, el)
    m2 = jnp.max(el2, axis=-1, keepdims=True)
    i2 = first_argmax(el2, m2)
    p1 = 1.0 / es
    p2 = jnp.exp(m2 - m1) / es
    den = p1 + p2
    w1 = g_w * (p1 / den)
    w2 = g_w * (p2 / den)
    sel1 = lane == i1
    sel2 = lane == i2
    onehot = jnp.where(sel1 | sel2, 1.0, 0.0)
    r_i = lax.broadcasted_iota(I32, (tm, tm), 0)
    c_i = lax.broadcasted_iota(I32, (tm, tm), 1)
    before = jnp.where(r_i > c_i, 1.0, 0.0).astype(BF16)
    base = carry[0:1, :] + _dot(before, onehot.astype(BF16))
    r1 = jnp.sum(jnp.where(sel1, base, 0.0), axis=-1, keepdims=True)
    r2 = jnp.sum(jnp.where(sel2, base, 0.0), axis=-1, keepdims=True)
    carry[0:1, :] = carry[0:1, :] + jnp.sum(onehot, axis=0, keepdims=True)
    cnt_ref[...] = carry[...]
    cols = [(i1 - EXPERT_LANE0).astype(F32), (i2 - EXPERT_LANE0).astype(F32), r1, r2, w1, w2]
    route = jnp.zeros((tm, ROUTE_LANES), F32)
    for n, cval in enumerate(cols):
        route = jnp.where(lane == n, cval, route)
    route_ref[...] = route


def _router(x, g, wr, br, *, tm=512):
    m, k = x.shape
    return pl.pallas_call(
        _router_kernel,
        grid=(m // tm,),
        in_specs=[pl.BlockSpec((tm, k), lambda i: (i, 0)),
                  pl.BlockSpec((1, k), lambda i: (0, 0)),
                  pl.BlockSpec((k, 2 * ROUTE_LANES), lambda i: (0, 0)),
                  pl.BlockSpec((1, ROUTE_LANES), lambda i: (0, 0))],
        out_specs=[pl.BlockSpec((tm * TOKEN_PITCH, LANES), lambda i: (i, 0)),
                   pl.BlockSpec((tm, ROUTE_LANES), lambda i: (i, 0)),
                   pl.BlockSpec((8, ROUTE_LANES), lambda i: (0, 0))],
        out_shape=[jax.ShapeDtypeStruct((m * TOKEN_PITCH, LANES), F32),
                   jax.ShapeDtypeStruct((m, ROUTE_LANES), F32),
                   jax.ShapeDtypeStruct((8, ROUTE_LANES), F32)],
        scratch_shapes=[pltpu.VMEM((8, ROUTE_LANES), F32)],
        compiler_params=_cparams(1),
        name="router",
    )(x, g, wr, br)


def _store_token_rows(ref, v):
    n = v.shape[0]
    for c in range(TOKEN_CHUNKS):
        ref[pl.ds(c, n, stride=TOKEN_PITCH), :] = v[:, c * LANES:(c + 1) * LANES]
    for c in range(TOKEN_CHUNKS, TOKEN_PITCH):
        ref[pl.ds(c, n, stride=TOKEN_PITCH), :] = jnp.zeros((n, LANES), v.dtype)


def _load_token_chunk(ref, first_token, n, c):
    return ref[pl.ds(first_token * TOKEN_PITCH + c, n, stride=TOKEN_PITCH), :]


def _token_copy(src_hbm, t, dst, r, sem):
    return pltpu.make_async_copy(src_hbm.at[pl.ds(t * TOKEN_PITCH, TOKEN_CHUNKS), :],
                                 dst.at[pl.ds(r * TOKEN_PITCH, TOKEN_CHUNKS), :], sem)


def _gather_tokens(idx_ref, n_tokens, src_hbm, dst, sem, *, unrolled):
    if unrolled:
        for r in range(n_tokens):
            _token_copy(src_hbm, idx_ref[0, 0, r], dst, r, sem).start()
    else:
        def body(r, carry):
            _token_copy(src_hbm, idx_ref[0, 0, r], dst, r, sem).start()
            return carry

        lax.fori_loop(0, n_tokens, body, 0)


def _wait_tokens(src_hbm, n_tokens, dst, sem):
    rows = n_tokens * TOKEN_CHUNKS
    pltpu.make_async_copy(src_hbm.at[pl.ds(0, rows), :], dst.at[pl.ds(0, rows), :], sem).wait()


def _expert_kernel(be_ref, nu_ref, src_cur, src_nxt, h_hbm, wg_ref, wu_ref, wd_ref, o_ref,
                   xbuf, sem, xs, wg_b, wu_b, wd_b):
    b = pl.program_id(0)
    n_used = nu_ref[0]
    slot = b % 2

    @pl.when(b == 0)
    def _():
        _gather_tokens(src_cur, EXPERT_BLOCK, h_hbm, xbuf.at[0], sem.at[0], unrolled=False)

    @pl.when(b + 1 < n_used)
    def _():
        _gather_tokens(src_nxt, EXPERT_BLOCK, h_hbm, xbuf.at[1 - slot], sem.at[1 - slot], unrolled=True)

    @pl.when(b < n_used)
    def _():
        @pl.when((b == 0) | (be_ref[b] != be_ref[jnp.maximum(b - 1, 0)]))
        def _():
            wg_b[...] = wg_ref[0].astype(BF16)
            wu_b[...] = wu_ref[0].astype(BF16)
            wd_b[...] = wd_ref[0].astype(BF16)

        _wait_tokens(h_hbm, EXPERT_BLOCK, xbuf.at[slot], sem.at[slot])
        for c in range(TOKEN_CHUNKS):
            xs[:, c * LANES:(c + 1) * LANES] = _load_token_chunk(xbuf.at[slot], 0, EXPERT_BLOCK, c).astype(BF16)
        x = xs[...]
        hid = (_silu(_dot(x, wg_b[...])) * _dot(x, wu_b[...])).astype(BF16)
        _store_token_rows(o_ref, _dot(hid, wd_b[...]))

    @pl.when(b >= n_used)
    def _():
        o_ref[...] = jnp.zeros(o_ref.shape, F32)


def _expert_ffn(block_expert, n_used, src, h, w_gate, w_up, w_down, *, layer):
    nb = block_expert.shape[0]
    d = TOKEN_CHUNKS * LANES
    blk_rows = EXPERT_BLOCK * TOKEN_PITCH
    nxt = lambda b, be, nu: (jnp.minimum(b + 1, nb - 1), 0, 0)
    wsel = lambda b, be, nu: (layer * N_EXPERTS + be[b], 0, 0)
    grid_spec = pltpu.PrefetchScalarGridSpec(
        num_scalar_prefetch=2,
        grid=(nb,),
        in_specs=[pl.BlockSpec((1, 1, EXPERT_BLOCK), lambda b, be, nu: (b, 0, 0), memory_space=pltpu.SMEM),
                  pl.BlockSpec((1, 1, EXPERT_BLOCK), nxt, memory_space=pltpu.SMEM),
                  pl.BlockSpec(memory_space=pl.ANY),
                  pl.BlockSpec((1, d, D_EXPERT), wsel),
                  pl.BlockSpec((1, d, D_EXPERT), wsel),
                  pl.BlockSpec((1, D_EXPERT, d), wsel)],
        out_specs=pl.BlockSpec((blk_rows, LANES), lambda b, be, nu: (b, 0)),
        scratch_shapes=[pltpu.VMEM((2, blk_rows, LANES), F32),
                        pltpu.SemaphoreType.DMA((2,)),
                        pltpu.VMEM((EXPERT_BLOCK, d), BF16),
                        pltpu.VMEM((d, D_EXPERT), BF16),
                        pltpu.VMEM((d, D_EXPERT), BF16),
                        pltpu.VMEM((D_EXPERT, d), BF16)])
    return pl.pallas_call(
        _expert_kernel,
        grid_spec=grid_spec,
        out_shape=jax.ShapeDtypeStruct((nb * blk_rows, LANES), F32),
        compiler_params=_cparams(1, disable_bounds_checks=True),
        name="expert_ffn",
    )(block_expert, n_used, src, src, h, w_gate, w_up, w_down)


def _combine_kernel(pos_cur, pos_nxt, x_ref, route_ref, y_hbm, o_ref, ybuf, sem):
    i = pl.program_id(0)
    n = pl.num_programs(0)
    tm = x_ref.shape[0]
    slot = i % 2

    @pl.when(i == 0)
    def _():
        _gather_tokens(pos_cur, TOP_K * tm, y_hbm, ybuf.at[0], sem.at[0], unrolled=False)

    @pl.when(i + 1 < n)
    def _():
        _gather_tokens(pos_nxt, TOP_K * tm, y_hbm, ybuf.at[1 - slot], sem.at[1 - slot], unrolled=True)

    _wait_tokens(y_hbm, TOP_K * tm, ybuf.at[slot], sem.at[slot])
    r = route_ref[...]
    g0 = r[:, 4:5]
    g1 = r[:, 5:6]
    for c in range(TOKEN_CHUNKS):
        cs_ = slice(c * LANES, (c + 1) * LANES)
        y = g0 * _load_token_chunk(ybuf.at[slot], 0, tm, c) + g1 * _load_token_chunk(ybuf.at[slot], tm, tm, c)
        o_ref[:, cs_] = x_ref[:, cs_] + y


def _combine(pos, x, route, yb, *, tm=256):
    m, d = x.shape
    nt = m // tm
    return pl.pallas_call(
        _combine_kernel,
        grid=(nt,),
        in_specs=[pl.BlockSpec((1, 1, TOP_K * tm), lambda i: (i, 0, 0), memory_space=pltpu.SMEM),
                  pl.BlockSpec((1, 1, TOP_K * tm), lambda i: (jnp.minimum(i + 1, nt - 1), 0, 0),
                               memory_space=pltpu.SMEM),
                  pl.BlockSpec((tm, d), lambda i: (i, 0)),
                  pl.BlockSpec((tm, ROUTE_LANES), lambda i: (i, 0)),
                  pl.BlockSpec(memory_space=pl.ANY)],
        out_specs=pl.BlockSpec((tm, d), lambda i: (i, 0)),
        out_shape=jax.ShapeDtypeStruct((m, d), F32),
        scratch_shapes=[pltpu.VMEM((2, TOP_K * tm * TOKEN_PITCH, LANES), F32),
                        pltpu.SemaphoreType.DMA((2,))],
        compiler_params=_cparams(1, disable_bounds_checks=True),
        name="combine",
    )(pos, pos, x, route, yb)


def _hier_moe(x, g, wr, br, w_gate, w_up, w_down, *, layer, combine_tm=256):
    n_tok = x.shape[0]
    h, route, cnt = _router(x, g, wr, br)
    counts = cnt[0, EXPERT_LANE0:EXPERT_LANE0 + N_EXPERTS].astype(I32)
    nblk = (counts + EXPERT_BLOCK - 1) // EXPERT_BLOCK
    bend = jnp.cumsum(nblk)
    bstart = bend - nblk
    nb = n_tok * TOP_K // EXPERT_BLOCK + N_EXPERTS
    expert = route[:, 0:TOP_K].astype(I32)
    rank = route[:, TOP_K:2 * TOP_K].astype(I32)
    pos = bstart[expert] * EXPERT_BLOCK + rank
    tok = jnp.broadcast_to(jnp.arange(n_tok, dtype=I32)[:, None], pos.shape)
    src = jnp.zeros((nb * EXPERT_BLOCK,), I32).at[pos.reshape(-1)].set(tok.reshape(-1))
    block_expert = jnp.sum((jnp.arange(nb, dtype=I32)[:, None] >= bend[None, :]).astype(I32), axis=1)
    block_expert = jnp.minimum(block_expert, N_EXPERTS - 1)
    yb = _expert_ffn(block_expert, bend[-1:].astype(I32), src.reshape(nb, 1, EXPERT_BLOCK),
                     h, w_gate, w_up, w_down, layer=layer)
    pos_tiles = pos.reshape(n_tok // combine_tm, combine_tm, TOP_K).transpose(0, 2, 1)
    pos_tiles = pos_tiles.reshape(n_tok // combine_tm, 1, TOP_K * combine_tm)
    return _combine(pos_tiles, x, route, yb, tm=combine_tm)


def _rope_tables(seqlen):
    inv_freq = ROPE_THETA ** (-jnp.arange(0, QK_ROPE, 2, dtype=F32) / QK_ROPE)
    ang = jnp.arange(seqlen, dtype=F32)[:, None] * inv_freq[None, :]
    cos, sin = jnp.cos(ang), jnp.sin(ang)
    zero = jnp.zeros((seqlen, LANES - QK_ROPE), F32)
    return (jnp.concatenate([cos, cos, zero], axis=-1), jnp.concatenate([-sin, sin, zero], axis=-1))


def _swap_halves(w):
    half = w.shape[-1] // 2
    return jnp.concatenate([w[..., half:], w[..., :half]], axis=-1)


def _router_weights(rg_w, rg_b, re_w, re_b):
    d = rg_w.shape[0]
    w = jnp.concatenate([rg_w, re_w.transpose(1, 0, 2).reshape(d, N_EXPERTS)], axis=1)
    b = jnp.concatenate([rg_b, re_b.reshape(N_EXPERTS)])
    pad = ROUTE_LANES - w.shape[1]
    w = jnp.pad(w, ((0, 0), (0, pad)))
    w_hi = w.astype(BF16)
    w_lo = (w - w_hi.astype(F32)).astype(BF16)
    return jnp.concatenate([w_hi, w_lo], axis=1), jnp.pad(b, (0, pad)).reshape(1, ROUTE_LANES)


def _row(v, width=None):
    v = v.reshape(1, -1).astype(F32)
    if width is not None:
        v = jnp.pad(v, ((0, 0), (0, width - v.shape[1])))
    return v


def kernel(x, ssm_norm_g, ssm_in_w, ssm_conv_w, ssm_conv_b, ssm_dt_bias, ssm_A_log, ssm_D, ssm_gate_norm_g, ssm_out_w, kv_norm_g, kv_w_dkv, kv_latent_g, kv_w_uk, kv_w_uv, kv_w_kr, attn_norm_g, q_w_dq, q_latent_g, q_w_uq, attn_w_o, ffn_norm_g, router_group_w, router_group_b, router_expert_w, router_expert_b, expert_w_gate, expert_w_up, expert_w_down, final_norm_g):
    bsz, seqlen, d = x.shape
    n_tok = bsz * seqlen
    xs = x.reshape(n_tok, d)
    n_ssm = ssm_in_w.shape[0]
    depth = ffn_norm_g.shape[0]
    ck, sk = _rope_tables(seqlen)
    zx_cols = D_INNER + CONV_DIM
    k_all = v_all = None
    w_gate_all = expert_w_gate.reshape(depth * N_EXPERTS, d, D_EXPERT)
    w_up_all = expert_w_up.reshape(depth * N_EXPERTS, d, D_EXPERT)
    w_down_all = expert_w_down.reshape(depth * N_EXPERTS, D_EXPERT, d)

    for layer in range(depth):
        if layer < n_ssm:
            i = layer
            w_in = ssm_in_w[i].astype(BF16)
            w_dt = jnp.pad(w_in[:, zx_cols:], ((0, 0), (0, LANES - SSM_HEADS)))
            zx, dt_raw = _in_proj(xs, _row(ssm_norm_g[i]), w_in, w_dt)
            yn = _ssd(zx, dt_raw, ssm_conv_w[i], _row(ssm_conv_b[i]), _row(ssm_dt_bias[i], LANES),
                      _row(ssm_A_log[i], LANES), _row(jnp.repeat(ssm_D[i], SSM_HEADDIM)),
                      _row(ssm_gate_norm_g[i]), bsz=bsz, seqlen=seqlen)
            xs = _mm_res(yn, ssm_out_w[i].astype(BF16), xs)
        else:
            j = layer - n_ssm
            if k_all is None:
                wkr2 = jnp.concatenate([kv_w_kr, _swap_halves(kv_w_kr)], axis=1).astype(BF16)
                k_all, v_all = _shared_kv(xs, _row(kv_norm_g), kv_w_dkv.astype(BF16), _row(kv_latent_g),
                                          kv_w_uk.astype(BF16), kv_w_uv.astype(BF16), wkr2, ck, sk,
                                          seqlen=seqlen)
            wq = q_w_uq[j].reshape(Q_LORA, MLA_HEADS, QK_NOPE + QK_ROPE)
            wn = wq[:, :, :QK_NOPE].reshape(Q_LORA, MLA_HEADS * QK_NOPE).astype(BF16)
            wr = wq[:, :, QK_NOPE:]
            lane_pad = ((0, 0), (0, 0), (0, LANES - QK_ROPE))
            wr_p = jnp.pad(wr, lane_pad).reshape(Q_LORA, MLA_HEADS * LANES).astype(BF16)
            ws_p = jnp.pad(_swap_halves(wr), lane_pad).reshape(Q_LORA, MLA_HEADS * LANES).astype(BF16)
            q_all = _q_side(xs, _row(attn_norm_g[j]), q_w_dq[j].astype(BF16), _row(q_latent_g[j]),
                            wn, wr_p, ws_p, ck, sk, seqlen=seqlen)
            o = _attention(q_all, k_all, v_all, bsz=bsz, seqlen=seqlen)
            xs = _mm_res(o, attn_w_o[j].astype(BF16), xs)
        wr_l, br_l = _router_weights(router_group_w[layer], router_group_b[layer],
                                     router_expert_w[layer], router_expert_b[layer])
        xs = _hier_moe(xs, _row(ffn_norm_g[layer]), wr_l, br_l, w_gate_all, w_up_all, w_down_all, layer=layer)
    return _final_norm(xs, _row(final_norm_g)).reshape(bsz, seqlen, d)
```

```python
import functools
import math

import jax
import jax.numpy as jnp
from jax import lax
from jax.experimental import pallas as pl
from jax.experimental.pallas import tpu as pltpu

F32 = jnp.float32
BF16 = jnp.bfloat16
I32 = jnp.int32
U32 = jnp.uint32
HIGHEST = lax.Precision.HIGHEST

EPS = 1e-6
D_MODEL = 2048

D_INNER = 4096
SSM_HEADDIM = 64
SSM_HEADS = 64
SSM_GROUPS = 8
SSM_STATE = 128
CONV_WIDTH = 4
CHUNK = 128
CONV_DIM = D_INNER + 2 * SSM_GROUPS * SSM_STATE
GROUP_CH = D_INNER // SSM_GROUPS
CONV_HALO = 8

MLA_HEADS = 16
Q_LORA = 512
KV_LORA = 512
QK_NOPE = 128
QK_ROPE = 64
V_HEAD = 128
ROPE_THETA = 10000.0
QK_PAD = 256
ATT_TQ = 512
ATT_TK = 512
ATT_HEADS_PER_STEP = 2
SCORE_SCALE = (QK_NOPE + QK_ROPE) ** -0.5 * math.log2(math.e)

N_GROUPS = 4
EXPERTS_PER_GROUP = 8
N_EXPERTS = 32
TOP_K = 2
D_EXPERT = 512
ROUTE_LANES = 128
EXPERT_LANE0 = N_GROUPS
EXPERT_BLOCK = 256
TOKEN_WORDS = D_MODEL // 256
TOKEN_PITCH = TOKEN_WORDS + 1
HIGH_HALF = 0xFFFF0000

LANES = 128
VMEM_LIMIT = 56 * 1024 * 1024


def _cparams(n_axes, **kw):
    return pltpu.CompilerParams(dimension_semantics=("arbitrary",) * n_axes,
                                vmem_limit_bytes=VMEM_LIMIT, **kw)


def _rms(xf, g):
    return xf * lax.rsqrt(jnp.mean(xf * xf, axis=-1, keepdims=True) + EPS) * g


def _dot(a, b):
    return jnp.dot(a, b, preferred_element_type=F32)


def _silu(v):
    return v * jax.nn.sigmoid(v)


def _mm_res_kernel(a_ref, w_ref, r_ref, o_ref):
    o_ref[...] = r_ref[...] + _dot(a_ref[...], w_ref[...])


def _mm_res(a, w, res, *, tm=1024):
    m, k = a.shape
    n = w.shape[1]
    tm = min(tm, m)
    tn = min(n, 1024 if k <= 2048 else 512)
    return pl.pallas_call(
        _mm_res_kernel,
        grid=(m // tm, n // tn),
        in_specs=[pl.BlockSpec((tm, k), lambda i, j: (i, 0)),
                  pl.BlockSpec((k, tn), lambda i, j: (0, j)),
                  pl.BlockSpec((tm, tn), lambda i, j: (i, j))],
        out_specs=pl.BlockSpec((tm, tn), lambda i, j: (i, j)),
        out_shape=jax.ShapeDtypeStruct((m, n), F32),
        compiler_params=_cparams(2),
        name="mm_res",
    )(a, w, res)


def _in_proj_kernel(x_ref, g_ref, w_ref, wdt_ref, o_ref, odt_ref, h_scr):
    @pl.when(pl.program_id(1) == 0)
    def _():
        h = _rms(x_ref[...], g_ref[...]).astype(BF16)
        h_scr[...] = h
        odt_ref[...] = _dot(h, wdt_ref[...])

    o_ref[...] = _dot(h_scr[...], w_ref[...]).astype(o_ref.dtype)


def _in_proj(x, g, w, wdt, *, tm=1024, tn=512):
    m, k = x.shape
    n = w.shape[1] // tn * tn
    tm = min(tm, m)
    return pl.pallas_call(
        _in_proj_kernel,
        grid=(m // tm, n // tn),
        in_specs=[pl.BlockSpec((tm, k), lambda i, j: (i, 0)),
                  pl.BlockSpec((1, k), lambda i, j: (0, 0)),
                  pl.BlockSpec((k, tn), lambda i, j: (0, j)),
                  pl.BlockSpec((k, LANES), lambda i, j: (0, 0))],
        out_specs=[pl.BlockSpec((tm, tn), lambda i, j: (i, j)),
                   pl.BlockSpec((tm, LANES), lambda i, j: (i, 0))],
        out_shape=[jax.ShapeDtypeStruct((m, n), BF16),
                   jax.ShapeDtypeStruct((m, LANES), F32)],
        scratch_shapes=[pltpu.VMEM((tm, k), BF16)],
        compiler_params=_cparams(2),
        name="in_proj",
    )(x, g, w, wdt)


def _softplus(v):
    return jnp.maximum(v, 0.0) + jnp.log1p(jnp.exp(-jnp.abs(v)))


def _ssd_kernel(z_ref, x_ref, bc_ref, dt_ref, cw_ref, cb_ref, dtb_ref, alog_ref, dexp_ref, gn_ref,
                y_ref, ubuf, act, xw, st):
    L = CHUNK

    @pl.when(pl.program_id(1) == 0)
    def _():
        ubuf[0:CONV_HALO, :] = jnp.zeros((CONV_HALO, CONV_DIM), F32)
        st[...] = jnp.zeros(st.shape, F32)

    ubuf[CONV_HALO:CONV_HALO + L, 0:D_INNER] = x_ref[...].astype(F32)
    ubuf[CONV_HALO:CONV_HALO + L, D_INNER:CONV_DIM] = bc_ref[...].astype(F32)
    cblk = 512
    for j in range(CONV_DIM // cblk):
        cs_ = slice(j * cblk, (j + 1) * cblk)
        acc = cb_ref[:, cs_] + cw_ref[0:1, cs_] * ubuf[CONV_HALO - 3:CONV_HALO - 3 + L, cs_]
        for k in range(1, CONV_WIDTH):
            r0 = CONV_HALO - 3 + k
            acc = acc + cw_ref[k:k + 1, cs_] * ubuf[r0:r0 + L, cs_]
        act[:, cs_] = _silu(acc)
    ubuf[0:CONV_HALO, :] = ubuf[L:L + CONV_HALO, :]

    dtv = _softplus(dt_ref[...] + dtb_ref[...])
    a = dtv * (-jnp.exp(alog_ref[...]))
    row = lax.broadcasted_iota(I32, (L, L), 0)
    col = lax.broadcasted_iota(I32, (L, L), 1)
    causal = row >= col
    cs = jnp.dot(causal.astype(F32), a, precision=HIGHEST, preferred_element_type=F32)
    cs_t = cs.T
    dt_t = dtv.T
    ecs = jnp.exp(cs)
    wgt = dtv * jnp.exp(cs[L - 1:L, :] - cs)
    first_head = col < SSM_HEADDIM

    def pair(v, h0):
        return jnp.where(first_head, v[:, h0:h0 + 1], v[:, h0 + 1:h0 + 2])

    for g in range(SSM_GROUPS):
        b0 = D_INNER + g * SSM_STATE
        c0 = D_INNER + SSM_GROUPS * SSM_STATE + g * SSM_STATE
        bg = act[:, b0:b0 + SSM_STATE].astype(BF16)
        cg = act[:, c0:c0 + SSM_STATE].astype(BF16)
        cb = lax.dot_general(cg, bg, (((1,), (1,)), ((), ())), preferred_element_type=F32)
        st_g = st[g]
        y_off = _dot(cg, st_g.astype(BF16))
        for qq in range(GROUP_CH // LANES):
            lo = g * GROUP_CH + qq * LANES
            h0 = lo // SSM_HEADDIM
            xf = act[:, lo:lo + LANES]
            xb = xf.astype(BF16)

            def head_mat(h):
                dec = jnp.exp(jnp.where(causal, cs[:, h:h + 1] - cs_t[h:h + 1, :], -jnp.inf))
                return (cb * dec * dt_t[h:h + 1, :]).astype(BF16)

            y_diag = jnp.where(first_head, _dot(head_mat(h0), xb), _dot(head_mat(h0 + 1), xb))
            e_pair = pair(ecs, h0)
            y = y_diag + y_off[:, qq * LANES:(qq + 1) * LANES] * e_pair + xf * dexp_ref[:, lo:lo + LANES]
            zf = z_ref[:, lo:lo + LANES].astype(F32)
            act[:, lo:lo + LANES] = y * _silu(zf)
            xw[:, qq * LANES:(qq + 1) * LANES] = (xf * pair(wgt, h0)).astype(BF16)
            st[g, :, qq * LANES:(qq + 1) * LANES] = (
                st_g[:, qq * LANES:(qq + 1) * LANES] * e_pair[L - 1:L, :])
        upd = lax.dot_general(bg, xw[...], (((0,), (0,)), ((), ())), preferred_element_type=F32)
        st[g] = st[g] + upd
        gs = slice(g * GROUP_CH, (g + 1) * GROUP_CH)
        y_ref[:, gs] = _rms(act[:, gs], gn_ref[:, gs]).astype(y_ref.dtype)


def _ssd(zx, dt_raw, conv_w, conv_b, dt_bias, a_log, d_exp, gate_g, *, bsz, seqlen):
    nc = seqlen // CHUNK
    row = lambda b, c: b * nc + c
    full = lambda shape: pl.BlockSpec(shape, lambda b, c: (0,) * len(shape))
    x_blk = D_INNER // D_INNER
    bc_blk = (2 * D_INNER) // (CONV_DIM - D_INNER)
    return pl.pallas_call(
        _ssd_kernel,
        grid=(bsz, nc),
        in_specs=[pl.BlockSpec((CHUNK, D_INNER), lambda b, c: (row(b, c), 0)),
                  pl.BlockSpec((CHUNK, D_INNER), lambda b, c: (row(b, c), x_blk)),
                  pl.BlockSpec((CHUNK, CONV_DIM - D_INNER), lambda b, c: (row(b, c), bc_blk)),
                  pl.BlockSpec((CHUNK, LANES), lambda b, c: (row(b, c), 0)),
                  full((CONV_WIDTH, CONV_DIM)), full((1, CONV_DIM)), full((1, LANES)), full((1, LANES)),
                  full((1, D_INNER)), full((1, D_INNER))],
        out_specs=pl.BlockSpec((CHUNK, D_INNER), lambda b, c: (row(b, c), 0)),
        out_shape=jax.ShapeDtypeStruct((bsz * seqlen, D_INNER), BF16),
        scratch_shapes=[pltpu.VMEM((CHUNK + CONV_HALO, CONV_DIM), F32),
                        pltpu.VMEM((CHUNK, CONV_DIM), F32),
                        pltpu.VMEM((CHUNK, GROUP_CH), BF16),
                        pltpu.VMEM((SSM_GROUPS, SSM_STATE, GROUP_CH), F32)],
        compiler_params=_cparams(2),
        name="ssd",
    )(zx, zx, zx, dt_raw, conv_w, conv_b, dt_bias, a_log, d_exp, gate_g)


def _kv_kernel(x_ref, g_ref, wdkv_ref, lg_ref, wuk_ref, wuv_ref, wkr_ref, ck_ref, sk_ref,
               k_ref, v_ref):
    h = _rms(x_ref[...], g_ref[...]).astype(BF16)
    ckv = _rms(_dot(h, wdkv_ref[...]), lg_ref[...]).astype(BF16)
    kn = _dot(ckv, wuk_ref[...]).astype(BF16)
    v_ref[...] = _dot(ckv, wuv_ref[...]).astype(BF16)
    kk = _dot(h, wkr_ref[...])
    kr = (kk * ck_ref[...] + pltpu.roll(kk, QK_ROPE, 1) * sk_ref[...]).astype(BF16)
    for hd in range(MLA_HEADS):
        k_ref[:, hd * QK_PAD:hd * QK_PAD + QK_NOPE] = kn[:, hd * QK_NOPE:(hd + 1) * QK_NOPE]
        k_ref[:, hd * QK_PAD + QK_NOPE:(hd + 1) * QK_PAD] = kr


def _shared_kv(x, g, wdkv, lg, wuk, wuv, wkr2, ck, sk, *, seqlen, tm=512):
    m, k = x.shape
    nt = seqlen // tm
    full = lambda a: pl.BlockSpec(a.shape, lambda i: (0,) * a.ndim)
    tab = pl.BlockSpec((tm, LANES), lambda i: (i % nt, 0))
    return pl.pallas_call(
        _kv_kernel,
        grid=(m // tm,),
        in_specs=[pl.BlockSpec((tm, k), lambda i: (i, 0)), full(g), full(wdkv), full(lg), full(wuk),
                  full(wuv), full(wkr2), tab, tab],
        out_specs=[pl.BlockSpec((tm, MLA_HEADS * QK_PAD), lambda i: (i, 0)),
                   pl.BlockSpec((tm, MLA_HEADS * V_HEAD), lambda i: (i, 0))],
        out_shape=[jax.ShapeDtypeStruct((m, MLA_HEADS * QK_PAD), BF16),
                   jax.ShapeDtypeStruct((m, MLA_HEADS * V_HEAD), BF16)],
        compiler_params=_cparams(1),
        name="shared_kv",
    )(x, g, wdkv, lg, wuk, wuv, wkr2, ck, sk)


def _q_kernel(x_ref, g_ref, wdq_ref, lg_ref, wn_ref, wr_ref, ws_ref, ck_ref, sk_ref, q_ref):
    h = _rms(x_ref[...], g_ref[...]).astype(BF16)
    cq = _rms(_dot(h, wdq_ref[...]), lg_ref[...]).astype(BF16)
    qn = (_dot(cq, wn_ref[...]) * SCORE_SCALE).astype(BF16)
    qr = _dot(cq, wr_ref[...])
    qs = _dot(cq, ws_ref[...])
    ck = ck_ref[...]
    sk = sk_ref[...]
    for hd in range(MLA_HEADS):
        q_ref[:, hd * QK_PAD:hd * QK_PAD + QK_NOPE] = qn[:, hd * QK_NOPE:(hd + 1) * QK_NOPE]
        sl = slice(hd * LANES, (hd + 1) * LANES)
        q_ref[:, hd * QK_PAD + QK_NOPE:(hd + 1) * QK_PAD] = (
            (qr[:, sl] * ck + qs[:, sl] * sk) * SCORE_SCALE).astype(BF16)


def _q_side(x, g, wdq, lg, wn, wr, ws, ck, sk, *, seqlen, tm=512):
    m, k = x.shape
    nt = seqlen // tm
    full = lambda a: pl.BlockSpec(a.shape, lambda i: (0,) * a.ndim)
    tab = pl.BlockSpec((tm, LANES), lambda i: (i % nt, 0))
    return pl.pallas_call(
        _q_kernel,
        grid=(m // tm,),
        in_specs=[pl.BlockSpec((tm, k), lambda i: (i, 0)), full(g), full(wdq), full(lg), full(wn),
                  full(wr), full(ws), tab, tab],
        out_specs=pl.BlockSpec((tm, MLA_HEADS * QK_PAD), lambda i: (i, 0)),
        out_shape=jax.ShapeDtypeStruct((m, MLA_HEADS * QK_PAD), BF16),
        compiler_params=_cparams(1),
        name="q_side",
    )(x, g, wdq, lg, wn, wr, ws, ck, sk)


def _attn_kernel(q_ref, k_ref, v_ref, o_ref, m_scr, l_scr, acc_scr, s_scr):
    qi = pl.program_id(2)
    m_scr[...] = jnp.full(m_scr.shape, -jnp.inf, F32)
    l_scr[...] = jnp.zeros(l_scr.shape, F32)
    acc_scr[...] = jnp.zeros(acc_scr.shape, F32)

    def score(kc, slot):
        k0 = pl.multiple_of(kc * ATT_TK, ATT_TK)
        for hh in range(ATT_HEADS_PER_STEP):
            q = q_ref[:, hh * QK_PAD:(hh + 1) * QK_PAD]
            k = k_ref[pl.ds(k0, ATT_TK), hh * QK_PAD:(hh + 1) * QK_PAD]
            s_scr[slot, hh] = lax.dot_general(k, q, (((1,), (1,)), ((), ())), preferred_element_type=F32)

    def step(kc, slot, diagonal):
        k0 = pl.multiple_of(kc * ATT_TK, ATT_TK)
        for hh in range(ATT_HEADS_PER_STEP):
            v = v_ref[pl.ds(k0, ATT_TK), hh * V_HEAD:(hh + 1) * V_HEAD]
            s = s_scr[slot, hh]
            if diagonal:
                key = lax.broadcasted_iota(I32, (ATT_TK, ATT_TQ), 0)
                qry = lax.broadcasted_iota(I32, (ATT_TK, ATT_TQ), 1)
                s = jnp.where(key <= qry, s, -jnp.inf)
            m_prev = m_scr[hh]
            m_new = jnp.maximum(m_prev, jnp.max(s, axis=0, keepdims=True))
            alpha = jnp.exp2(m_prev - m_new)
            p = jnp.exp2(s - m_new)
            l_scr[hh] = alpha * l_scr[hh] + jnp.sum(p, axis=0, keepdims=True)
            pv = lax.dot_general(v, p.astype(BF16), (((0,), (0,)), ((), ())), preferred_element_type=F32)
            acc_scr[hh] = alpha * acc_scr[hh] + pv
            m_scr[hh] = m_new

    def pair(j, carry):
        c = 2 * j
        score(c + 1, 1)
        step(c, 0, False)
        score(c + 2, 0)
        step(c + 1, 1, False)
        return carry

    score(0, 0)
    lax.fori_loop(0, qi // 2, pair, 0)

    @pl.when(qi % 2 == 0)
    def _():
        step(qi, 0, True)

    @pl.when(qi % 2 == 1)
    def _():
        score(qi, 1)
        step(qi - 1, 0, False)
        step(qi, 1, True)

    for hh in range(ATT_HEADS_PER_STEP):
        o_ref[:, hh * V_HEAD:(hh + 1) * V_HEAD] = (acc_scr[hh] / l_scr[hh]).T.astype(o_ref.dtype)


def _attention(q, k, v, *, bsz, seqlen):
    nq = seqlen // ATT_TQ
    hps = ATT_HEADS_PER_STEP
    return pl.pallas_call(
        _attn_kernel,
        grid=(bsz, MLA_HEADS // hps, nq),
        in_specs=[pl.BlockSpec((ATT_TQ, hps * QK_PAD), lambda b, h, i: (b * nq + i, h)),
                  pl.BlockSpec((seqlen, hps * QK_PAD), lambda b, h, i: (b, h)),
                  pl.BlockSpec((seqlen, hps * V_HEAD), lambda b, h, i: (b, h))],
        out_specs=pl.BlockSpec((ATT_TQ, hps * V_HEAD), lambda b, h, i: (b * nq + i, h)),
        out_shape=jax.ShapeDtypeStruct((bsz * seqlen, MLA_HEADS * V_HEAD), BF16),
        scratch_shapes=[pltpu.VMEM((hps, 1, ATT_TQ), F32), pltpu.VMEM((hps, 1, ATT_TQ), F32),
                        pltpu.VMEM((hps, V_HEAD, ATT_TQ), F32),
                        pltpu.VMEM((2, hps, ATT_TK, ATT_TQ), F32)],
        compiler_params=_cparams(3),
        name="attention",
    )(q, k, v)


def _router_kernel(x_ref, g_ref, wr_ref, br_ref, h_ref, route_ref, cnt_ref, carry):
    tm = x_ref.shape[0]

    @pl.when(pl.program_id(0) == 0)
    def _():
        carry[...] = jnp.zeros(carry.shape, F32)

    h = _rms(x_ref[...], g_ref[...])
    _store_token_rows(h_ref, h)
    h_hi = h.astype(BF16)
    h_lo = (h - h_hi.astype(F32)).astype(BF16)
    hw = _dot(h_hi, wr_ref[...])
    logits = (hw[:, :ROUTE_LANES] + hw[:, ROUTE_LANES:] + _dot(h_lo, wr_ref[:, :ROUTE_LANES])) + br_ref[...]
    lane = lax.broadcasted_iota(I32, (tm, ROUTE_LANES), 1)
    neg = -jnp.inf

    def first_argmax(v, vmax):
        return jnp.min(jnp.where(v == vmax, lane, ROUTE_LANES), axis=-1, keepdims=True)

    gl = jnp.where(lane < N_GROUPS, logits, neg)
    gm = jnp.max(gl, axis=-1, keepdims=True)
    g_w = 1.0 / jnp.sum(jnp.exp(gl - gm), axis=-1, keepdims=True)
    gi = first_argmax(gl, gm)
    lo = EXPERT_LANE0 + gi * EXPERTS_PER_GROUP
    el = jnp.where((lane >= lo) & (lane < lo + EXPERTS_PER_GROUP), logits, neg)
    m1 = jnp.max(el, axis=-1, keepdims=True)
    es = jnp.sum(jnp.exp(el - m1), axis=-1, keepdims=True)
    i1 = first_argmax(el, m1)
    el2 = jnp.where(lane == i1, neg, el)
    m2 = jnp.max(el2, axis=-1, keepdims=True)
    i2 = first_argmax(el2, m2)
    p1 = 1.0 / es
    p2 = jnp.exp(m2 - m1) / es
    den = p1 + p2
    w1 = g_w * (p1 / den)
    w2 = g_w * (p2 / den)
    sel1 = lane == i1
    sel2 = lane == i2
    onehot = jnp.where(sel1 | sel2, 1.0, 0.0)
    r_i = lax.broadcasted_iota(I32, (tm, tm), 0)
    c_i = lax.broadcasted_iota(I32, (tm, tm), 1)
    before = jnp.where(r_i > c_i, 1.0, 0.0).astype(BF16)
    base = carry[0:1, :] + _dot(before, onehot.astype(BF16))
    r1 = jnp.sum(jnp.where(sel1, base, 0.0), axis=-1, keepdims=True)
    r2 = jnp.sum(jnp.where(sel2, base, 0.0), axis=-1, keepdims=True)
    carry[0:1, :] = carry[0:1, :] + jnp.sum(onehot, axis=0, keepdims=True)
    cnt_ref[...] = carry[...]
    cols = [(i1 - EXPERT_LANE0).astype(F32), (i2 - EXPERT_LANE0).astype(F32), r1, r2, w1, w2]
    route = jnp.zeros((tm, ROUTE_LANES), F32)
    for n, cval in enumerate(cols):
        route = jnp.where(lane == n, cval, route)
    route_ref[...] = route


def _router(x, g, wr, br, *, tm=512):
    m, k = x.shape
    return pl.pallas_call(
        _router_kernel,
        grid=(m // tm,),
        in_specs=[pl.BlockSpec((tm, k), lambda i: (i, 0)),
                  pl.BlockSpec((1, k), lambda i: (0, 0)),
                  pl.BlockSpec((k, 2 * ROUTE_LANES), lambda i: (0, 0)),
                  pl.BlockSpec((1, ROUTE_LANES), lambda i: (0, 0))],
        out_specs=[pl.BlockSpec((tm * TOKEN_PITCH, LANES), lambda i: (i, 0)),
                   pl.BlockSpec((tm, ROUTE_LANES), lambda i: (i, 0)),
                   pl.BlockSpec((8, ROUTE_LANES), lambda i: (0, 0))],
        out_shape=[jax.ShapeDtypeStruct((m * TOKEN_PITCH, LANES), U32),
                   jax.ShapeDtypeStruct((m, ROUTE_LANES), F32),
                   jax.ShapeDtypeStruct((8, ROUTE_LANES), F32)],
        scratch_shapes=[pltpu.VMEM((8, ROUTE_LANES), F32)],
        compiler_params=_cparams(1),
        name="router",
    )(x, g, wr, br)


def _pack_bf16_pair(lo, hi):
    lo_bits = pltpu.bitcast(lo.astype(BF16).astype(F32), U32) >> 16
    hi_bits = pltpu.bitcast(hi.astype(BF16).astype(F32), U32) & jnp.uint32(HIGH_HALF)
    return lo_bits | hi_bits


def _unpack_bf16_pair(w):
    return pltpu.bitcast(w << 16, F32), pltpu.bitcast(w & jnp.uint32(HIGH_HALF), F32)


def _store_token_rows(ref, v):
    n = v.shape[0]
    for c in range(TOKEN_WORDS):
        hi = c + TOKEN_WORDS
        ref[pl.ds(c, n, stride=TOKEN_PITCH), :] = _pack_bf16_pair(v[:, c * LANES:(c + 1) * LANES],
                                                                   v[:, hi * LANES:(hi + 1) * LANES])
    for c in range(TOKEN_WORDS, TOKEN_PITCH):
        ref[pl.ds(c, n, stride=TOKEN_PITCH), :] = jnp.zeros((n, LANES), U32)


def _load_token_chunks(ref, first_token, n, c):
    return _unpack_bf16_pair(ref[pl.ds(first_token * TOKEN_PITCH + c, n, stride=TOKEN_PITCH), :])


def _token_copy(src_hbm, t, dst, r, sem):
    return pltpu.make_async_copy(src_hbm.at[pl.ds(t * TOKEN_PITCH, TOKEN_WORDS), :],
                                 dst.at[pl.ds(r * TOKEN_PITCH, TOKEN_WORDS), :], sem)


def _gather_tokens(idx_ref, n_tokens, src_hbm, dst, sem, *, unrolled):
    if unrolled:
        for r in range(n_tokens):
            _token_copy(src_hbm, idx_ref[0, 0, r], dst, r, sem).start()
    else:
        def body(r, carry):
            _token_copy(src_hbm, idx_ref[0, 0, r], dst, r, sem).start()
            return carry

        lax.fori_loop(0, n_tokens, body, 0)


def _wait_tokens(src_hbm, n_tokens, dst, sem):
    rows = n_tokens * TOKEN_WORDS
    pltpu.make_async_copy(src_hbm.at[pl.ds(0, rows), :], dst.at[pl.ds(0, rows), :], sem).wait()


def _expert_kernel(be_ref, nu_ref, src_cur, src_nxt, h_hbm, wg_ref, wu_ref, wd_ref, o_ref,
                   xbuf, sem, xs, wg_b, wu_b, wd_b):
    b = pl.program_id(0)
    n_used = nu_ref[0]
    slot = b % 2

    @pl.when(b == 0)
    def _():
        _gather_tokens(src_cur, EXPERT_BLOCK, h_hbm, xbuf.at[0], sem.at[0], unrolled=False)

    @pl.when(b + 1 < n_used)
    def _():
        _gather_tokens(src_nxt, EXPERT_BLOCK, h_hbm, xbuf.at[1 - slot], sem.at[1 - slot], unrolled=True)

    @pl.when(b < n_used)
    def _():
        @pl.when((b == 0) | (be_ref[b] != be_ref[jnp.maximum(b - 1, 0)]))
        def _():
            wg_b[...] = wg_ref[0].astype(BF16)
            wu_b[...] = wu_ref[0].astype(BF16)
            wd_b[...] = wd_ref[0].astype(BF16)

        _wait_tokens(h_hbm, EXPERT_BLOCK, xbuf.at[slot], sem.at[slot])
        for c in range(TOKEN_WORDS):
            lo, hi = _load_token_chunks(xbuf.at[slot], 0, EXPERT_BLOCK, c)
            xs[:, c * LANES:(c + 1) * LANES] = lo.astype(BF16)
            xs[:, (c + TOKEN_WORDS) * LANES:(c + TOKEN_WORDS + 1) * LANES] = hi.astype(BF16)
        x = xs[...]
        hid = (_silu(_dot(x, wg_b[...])) * _dot(x, wu_b[...])).astype(BF16)
        _store_token_rows(o_ref, _dot(hid, wd_b[...]))

    @pl.when(b >= n_used)
    def _():
        o_ref[...] = jnp.zeros(o_ref.shape, U32)


def _expert_ffn(block_expert, n_used, src, h, w_gate, w_up, w_down, *, layer):
    nb = block_expert.shape[0]
    d = D_MODEL
    blk_rows = EXPERT_BLOCK * TOKEN_PITCH
    nxt = lambda b, be, nu: (jnp.minimum(b + 1, nb - 1), 0, 0)
    wsel = lambda b, be, nu: (layer * N_EXPERTS + be[b], 0, 0)
    grid_spec = pltpu.PrefetchScalarGridSpec(
        num_scalar_prefetch=2,
        grid=(nb,),
        in_specs=[pl.BlockSpec((1, 1, EXPERT_BLOCK), lambda b, be, nu: (b, 0, 0), memory_space=pltpu.SMEM),
                  pl.BlockSpec((1, 1, EXPERT_BLOCK), nxt, memory_space=pltpu.SMEM),
                  pl.BlockSpec(memory_space=pl.ANY),
                  pl.BlockSpec((1, d, D_EXPERT), wsel),
                  pl.BlockSpec((1, d, D_EXPERT), wsel),
                  pl.BlockSpec((1, D_EXPERT, d), wsel)],
        out_specs=pl.BlockSpec((blk_rows, LANES), lambda b, be, nu: (b, 0)),
        scratch_shapes=[pltpu.VMEM((2, blk_rows, LANES), U32),
                        pltpu.SemaphoreType.DMA((2,)),
                        pltpu.VMEM((EXPERT_BLOCK, d), BF16),
                        pltpu.VMEM((d, D_EXPERT), BF16),
                        pltpu.VMEM((d, D_EXPERT), BF16),
                        pltpu.VMEM((D_EXPERT, d), BF16)])
    return pl.pallas_call(
        _expert_kernel,
        grid_spec=grid_spec,
        out_shape=jax.ShapeDtypeStruct((nb * blk_rows, LANES), U32),
        compiler_params=_cparams(1, disable_bounds_checks=True),
        name="expert_ffn",
    )(block_expert, n_used, src, src, h, w_gate, w_up, w_down)


def _combine_kernel(pos_cur, pos_nxt, x_ref, route_ref, fg_ref, y_hbm, o_ref, ybuf, sem, *, apply_final_norm):
    i = pl.program_id(0)
    n = pl.num_programs(0)
    tm = x_ref.shape[0]
    slot = i % 2

    @pl.when(i == 0)
    def _():
        _gather_tokens(pos_cur, TOP_K * tm, y_hbm, ybuf.at[0], sem.at[0], unrolled=False)

    @pl.when(i + 1 < n)
    def _():
        _gather_tokens(pos_nxt, TOP_K * tm, y_hbm, ybuf.at[1 - slot], sem.at[1 - slot], unrolled=True)

    _wait_tokens(y_hbm, TOP_K * tm, ybuf.at[slot], sem.at[slot])
    r = route_ref[...]
    g0 = r[:, 4:5]
    g1 = r[:, 5:6]
    for c in range(TOKEN_WORDS):
        first = _load_token_chunks(ybuf.at[slot], 0, tm, c)
        second = _load_token_chunks(ybuf.at[slot], tm, tm, c)
        for half in range(2):
            cs_ = slice((c + half * TOKEN_WORDS) * LANES, (c + half * TOKEN_WORDS + 1) * LANES)
            o_ref[:, cs_] = x_ref[:, cs_] + (g0 * first[half] + g1 * second[half])
    if apply_final_norm:
        o_ref[...] = _rms(o_ref[...], fg_ref[...])


def _combine(pos, x, route, yb, final_g, *, apply_final_norm, tm=256):
    m, d = x.shape
    nt = m // tm
    return pl.pallas_call(
        functools.partial(_combine_kernel, apply_final_norm=apply_final_norm),
        grid=(nt,),
        in_specs=[pl.BlockSpec((1, 1, TOP_K * tm), lambda i: (i, 0, 0), memory_space=pltpu.SMEM),
                  pl.BlockSpec((1, 1, TOP_K * tm), lambda i: (jnp.minimum(i + 1, nt - 1), 0, 0),
                               memory_space=pltpu.SMEM),
                  pl.BlockSpec((tm, d), lambda i: (i, 0)),
                  pl.BlockSpec((tm, ROUTE_LANES), lambda i: (i, 0)),
                  pl.BlockSpec((1, d), lambda i: (0, 0)),
                  pl.BlockSpec(memory_space=pl.ANY)],
        out_specs=pl.BlockSpec((tm, d), lambda i: (i, 0)),
        out_shape=jax.ShapeDtypeStruct((m, d), F32),
        scratch_shapes=[pltpu.VMEM((2, TOP_K * tm * TOKEN_PITCH, LANES), U32),
                        pltpu.SemaphoreType.DMA((2,))],
        compiler_params=_cparams(1, disable_bounds_checks=True),
        name="combine",
    )(pos, pos, x, route, final_g, yb)


def _hier_moe(x, g, wr, br, w_gate, w_up, w_down, final_g, *, layer, apply_final_norm, combine_tm=256):
    n_tok = x.shape[0]
    h, route, cnt = _router(x, g, wr, br)
    counts = cnt[0, EXPERT_LANE0:EXPERT_LANE0 + N_EXPERTS].astype(I32)
    nblk = (counts + EXPERT_BLOCK - 1) // EXPERT_BLOCK
    bend = jnp.cumsum(nblk)
    bstart = bend - nblk
    nb = n_tok * TOP_K // EXPERT_BLOCK + N_EXPERTS
    expert = route[:, 0:TOP_K].astype(I32)
    rank = route[:, TOP_K:2 * TOP_K].astype(I32)
    pos = bstart[expert] * EXPERT_BLOCK + rank
    tok = jnp.broadcast_to(jnp.arange(n_tok, dtype=I32)[:, None], pos.shape)
    src = jnp.zeros((nb * EXPERT_BLOCK,), I32).at[pos.reshape(-1)].set(tok.reshape(-1))
    block_expert = jnp.sum((jnp.arange(nb, dtype=I32)[:, None] >= bend[None, :]).astype(I32), axis=1)
    block_expert = jnp.minimum(block_expert, N_EXPERTS - 1)
    yb = _expert_ffn(block_expert, bend[-1:].astype(I32), src.reshape(nb, 1, EXPERT_BLOCK),
                     h, w_gate, w_up, w_down, layer=layer)
    pos_tiles = pos.reshape(n_tok // combine_tm, combine_tm, TOP_K).transpose(0, 2, 1)
    pos_tiles = pos_tiles.reshape(n_tok // combine_tm, 1, TOP_K * combine_tm)
    return _combine(pos_tiles, x, route, yb, final_g, apply_final_norm=apply_final_norm, tm=combine_tm)


def _rope_tables(seqlen):
    inv_freq = ROPE_THETA ** (-jnp.arange(0, QK_ROPE, 2, dtype=F32) / QK_ROPE)
    ang = jnp.arange(seqlen, dtype=F32)[:, None] * inv_freq[None, :]
    cos, sin = jnp.cos(ang), jnp.sin(ang)
    zero = jnp.zeros((seqlen, LANES - QK_ROPE), F32)
    return (jnp.concatenate([cos, cos, zero], axis=-1), jnp.concatenate([-sin, sin, zero], axis=-1))


def _swap_halves(w):
    half = w.shape[-1] // 2
    return jnp.concatenate([w[..., half:], w[..., :half]], axis=-1)


def _router_weights(rg_w, rg_b, re_w, re_b):
    d = rg_w.shape[0]
    w = jnp.concatenate([rg_w, re_w.transpose(1, 0, 2).reshape(d, N_EXPERTS)], axis=1)
    b = jnp.concatenate([rg_b, re_b.reshape(N_EXPERTS)])
    pad = ROUTE_LANES - w.shape[1]
    w = jnp.pad(w, ((0, 0), (0, pad)))
    w_hi = w.astype(BF16)
    w_lo = (w - w_hi.astype(F32)).astype(BF16)
    return jnp.concatenate([w_hi, w_lo], axis=1), jnp.pad(b, (0, pad)).reshape(1, ROUTE_LANES)


def _row(v, width=None):
    v = v.reshape(1, -1).astype(F32)
    if width is not None:
        v = jnp.pad(v, ((0, 0), (0, width - v.shape[1])))
    return v


def kernel(x, ssm_norm_g, ssm_in_w, ssm_conv_w, ssm_conv_b, ssm_dt_bias, ssm_A_log, ssm_D, ssm_gate_norm_g, ssm_out_w, kv_norm_g, kv_w_dkv, kv_latent_g, kv_w_uk, kv_w_uv, kv_w_kr, attn_norm_g, q_w_dq, q_latent_g, q_w_uq, attn_w_o, ffn_norm_g, router_group_w, router_group_b, router_expert_w, router_expert_b, expert_w_gate, expert_w_up, expert_w_down, final_norm_g):
    bsz, seqlen, d = x.shape
    n_tok = bsz * seqlen
    xs = x.reshape(n_tok, d)
    n_ssm = ssm_in_w.shape[0]
    depth = ffn_norm_g.shape[0]
    ck, sk = _rope_tables(seqlen)
    zx_cols = D_INNER + CONV_DIM
    k_all = v_all = None
    w_gate_all = expert_w_gate.reshape(depth * N_EXPERTS, d, D_EXPERT)
    w_up_all = expert_w_up.reshape(depth * N_EXPERTS, d, D_EXPERT)
    w_down_all = expert_w_down.reshape(depth * N_EXPERTS, D_EXPERT, d)

    for layer in range(depth):
        if layer < n_ssm:
            i = layer
            w_in = ssm_in_w[i].astype(BF16)
            w_dt = jnp.pad(w_in[:, zx_cols:], ((0, 0), (0, LANES - SSM_HEADS)))
            zx, dt_raw = _in_proj(xs, _row(ssm_norm_g[i]), w_in, w_dt)
            yn = _ssd(zx, dt_raw, ssm_conv_w[i], _row(ssm_conv_b[i]), _row(ssm_dt_bias[i], LANES),
                      _row(ssm_A_log[i], LANES), _row(jnp.repeat(ssm_D[i], SSM_HEADDIM)),
                      _row(ssm_gate_norm_g[i]), bsz=bsz, seqlen=seqlen)
            xs = _mm_res(yn, ssm_out_w[i].astype(BF16), xs)
        else:
            j = layer - n_ssm
            if k_all is None:
                wkr2 = jnp.concatenate([kv_w_kr, _swap_halves(kv_w_kr)], axis=1).astype(BF16)
                k_all, v_all = _shared_kv(xs, _row(kv_norm_g), kv_w_dkv.astype(BF16), _row(kv_latent_g),
                                          kv_w_uk.astype(BF16), kv_w_uv.astype(BF16), wkr2, ck, sk,
                                          seqlen=seqlen)
            wq = q_w_uq[j].reshape(Q_LORA, MLA_HEADS, QK_NOPE + QK_ROPE)
            wn = wq[:, :, :QK_NOPE].reshape(Q_LORA, MLA_HEADS * QK_NOPE).astype(BF16)
            wr = wq[:, :, QK_NOPE:]
            lane_pad = ((0, 0), (0, 0), (0, LANES - QK_ROPE))
            wr_p = jnp.pad(wr, lane_pad).reshape(Q_LORA, MLA_HEADS * LANES).astype(BF16)
            ws_p = jnp.pad(_swap_halves(wr), lane_pad).reshape(Q_LORA, MLA_HEADS * LANES).astype(BF16)
            q_all = _q_side(xs, _row(attn_norm_g[j]), q_w_dq[j].astype(BF16), _row(q_latent_g[j]),
                            wn, wr_p, ws_p, ck, sk, seqlen=seqlen)
            o = _attention(q_all, k_all, v_all, bsz=bsz, seqlen=seqlen)
            xs = _mm_res(o, attn_w_o[j].astype(BF16), xs)
        wr_l, br_l = _router_weights(router_group_w[layer], router_group_b[layer],
                                     router_expert_w[layer], router_expert_b[layer])
        xs = _hier_moe(xs, _row(ffn_norm_g[layer]), wr_l, br_l, w_gate_all, w_up_all, w_down_all,
                       _row(final_norm_g), layer=layer, apply_final_norm=(layer == depth - 1))
    return xs.reshape(bsz, seqlen, d)
```

```python
import functools
import math

import jax
import jax.numpy as jnp
from jax import lax
from jax.experimental import pallas as pl
from jax.experimental.pallas import tpu as pltpu

F32 = jnp.float32
BF16 = jnp.bfloat16
I32 = jnp.int32
U32 = jnp.uint32
HIGHEST = lax.Precision.HIGHEST

EPS = 1e-6
D_MODEL = 2048

D_INNER = 4096
SSM_HEADDIM = 64
SSM_HEADS = 64
SSM_GROUPS = 8
SSM_STATE = 128
CONV_WIDTH = 4
CHUNK = 128
CONV_DIM = D_INNER + 2 * SSM_GROUPS * SSM_STATE
GROUP_CH = D_INNER // SSM_GROUPS
CONV_HALO = 8

MLA_HEADS = 16
Q_LORA = 512
KV_LORA = 512
QK_NOPE = 128
QK_ROPE = 64
V_HEAD = 128
ROPE_THETA = 10000.0
QK_PAD = 256
ATT_TQ = 512
ATT_TK = 512
ATT_HEADS_PER_STEP = 2
SCORE_SCALE = (QK_NOPE + QK_ROPE) ** -0.5 * math.log2(math.e)

N_GROUPS = 4
EXPERTS_PER_GROUP = 8
N_EXPERTS = 32
TOP_K = 2
D_EXPERT = 512
ROUTE_LANES = 128
EXPERT_LANE0 = N_GROUPS
EXPERT_BLOCK = 256
TOKEN_WORDS = D_MODEL // 256
TOKEN_PITCH = TOKEN_WORDS + 1
HIGH_HALF = 0xFFFF0000

LANES = 128
VMEM_LIMIT = 56 * 1024 * 1024


def _cparams(n_axes, **kw):
    return pltpu.CompilerParams(dimension_semantics=("arbitrary",) * n_axes,
                                vmem_limit_bytes=VMEM_LIMIT, **kw)


def _rms(xf, g):
    return xf * lax.rsqrt(jnp.mean(xf * xf, axis=-1, keepdims=True) + EPS) * g


def _dot(a, b):
    return jnp.dot(a, b, preferred_element_type=F32)


def _silu(v):
    return v * jax.nn.sigmoid(v)


def _mm_res_kernel(a_ref, w_ref, r_ref, o_ref):
    o_ref[...] = r_ref[...] + _dot(a_ref[...], w_ref[...])


def _mm_res(a, w, res, *, tm=1024):
    m, k = a.shape
    n = w.shape[1]
    tm = min(tm, m)
    tn = min(n, 1024 if k <= 2048 else 512)
    return pl.pallas_call(
        _mm_res_kernel,
        grid=(m // tm, n // tn),
        in_specs=[pl.BlockSpec((tm, k), lambda i, j: (i, 0)),
                  pl.BlockSpec((k, tn), lambda i, j: (0, j)),
                  pl.BlockSpec((tm, tn), lambda i, j: (i, j))],
        out_specs=pl.BlockSpec((tm, tn), lambda i, j: (i, j)),
        out_shape=jax.ShapeDtypeStruct((m, n), F32),
        compiler_params=_cparams(2),
        name="mm_res",
    )(a, w, res)


def _in_proj_kernel(x_ref, g_ref, w_ref, wdt_ref, o_ref, odt_ref, h_scr):
    @pl.when(pl.program_id(1) == 0)
    def _():
        h = _rms(x_ref[...], g_ref[...]).astype(BF16)
        h_scr[...] = h
        odt_ref[...] = _dot(h, wdt_ref[...])

    o_ref[...] = _dot(h_scr[...], w_ref[...]).astype(o_ref.dtype)


def _in_proj(x, g, w, wdt, *, tm=1024, tn=512):
    m, k = x.shape
    n = w.shape[1] // tn * tn
    tm = min(tm, m)
    return pl.pallas_call(
        _in_proj_kernel,
        grid=(m // tm, n // tn),
        in_specs=[pl.BlockSpec((tm, k), lambda i, j: (i, 0)),
                  pl.BlockSpec((1, k), lambda i, j: (0, 0)),
                  pl.BlockSpec((k, tn), lambda i, j: (0, j)),
                  pl.BlockSpec((k, LANES), lambda i, j: (0, 0))],
        out_specs=[pl.BlockSpec((tm, tn), lambda i, j: (i, j)),
                   pl.BlockSpec((tm, LANES), lambda i, j: (i, 0))],
        out_shape=[jax.ShapeDtypeStruct((m, n), BF16),
                   jax.ShapeDtypeStruct((m, LANES), F32)],
        scratch_shapes=[pltpu.VMEM((tm, k), BF16)],
        compiler_params=_cparams(2),
        name="in_proj",
    )(x, g, w, wdt)


def _softplus(v):
    return jnp.maximum(v, 0.0) + jnp.log1p(jnp.exp(-jnp.abs(v)))


def _ssd_kernel(z_ref, x_ref, bc_ref, dt_ref, cw_ref, cb_ref, dtb_ref, alog_ref, dexp_ref, gn_ref,
                y_ref, ubuf, act, xw, st):
    L = CHUNK

    @pl.when(pl.program_id(1) == 0)
    def _():
        ubuf[0:CONV_HALO, :] = jnp.zeros((CONV_HALO, CONV_DIM), F32)
        st[...] = jnp.zeros(st.shape, F32)

    ubuf[CONV_HALO:CONV_HALO + L, 0:D_INNER] = x_ref[...].astype(F32)
    ubuf[CONV_HALO:CONV_HALO + L, D_INNER:CONV_DIM] = bc_ref[...].astype(F32)
    cblk = 512
    for j in range(CONV_DIM // cblk):
        cs_ = slice(j * cblk, (j + 1) * cblk)
        acc = cb_ref[:, cs_] + cw_ref[0:1, cs_] * ubuf[CONV_HALO - 3:CONV_HALO - 3 + L, cs_]
        for k in range(1, CONV_WIDTH):
            r0 = CONV_HALO - 3 + k
            acc = acc + cw_ref[k:k + 1, cs_] * ubuf[r0:r0 + L, cs_]
        act[:, cs_] = _silu(acc)
    ubuf[0:CONV_HALO, :] = ubuf[L:L + CONV_HALO, :]

    dtv = _softplus(dt_ref[...] + dtb_ref[...])
    a = dtv * (-jnp.exp(alog_ref[...]))
    row = lax.broadcasted_iota(I32, (L, L), 0)
    col = lax.broadcasted_iota(I32, (L, L), 1)
    causal = row >= col
    cs = jnp.dot(causal.astype(F32), a, precision=HIGHEST, preferred_element_type=F32)
    cs_t = cs.T
    dt_t = dtv.T
    ecs = jnp.exp(cs)
    wgt = dtv * jnp.exp(cs[L - 1:L, :] - cs)
    first_head = col < SSM_HEADDIM

    def pair(v, h0):
        return jnp.where(first_head, v[:, h0:h0 + 1], v[:, h0 + 1:h0 + 2])

    for g in range(SSM_GROUPS):
        b0 = D_INNER + g * SSM_STATE
        c0 = D_INNER + SSM_GROUPS * SSM_STATE + g * SSM_STATE
        bg = act[:, b0:b0 + SSM_STATE].astype(BF16)
        cg = act[:, c0:c0 + SSM_STATE].astype(BF16)
        cb = lax.dot_general(cg, bg, (((1,), (1,)), ((), ())), preferred_element_type=F32)
        st_g = st[g]
        y_off = _dot(cg, st_g.astype(BF16))
        for qq in range(GROUP_CH // LANES):
            lo = g * GROUP_CH + qq * LANES
            h0 = lo // SSM_HEADDIM
            xf = act[:, lo:lo + LANES]
            xb = xf.astype(BF16)

            def head_mat(h):
                dec = jnp.exp(jnp.where(causal, cs[:, h:h + 1] - cs_t[h:h + 1, :], -jnp.inf))
                return (cb * dec * dt_t[h:h + 1, :]).astype(BF16)

            y_diag = jnp.where(first_head, _dot(head_mat(h0), xb), _dot(head_mat(h0 + 1), xb))
            e_pair = pair(ecs, h0)
            y = y_diag + y_off[:, qq * LANES:(qq + 1) * LANES] * e_pair + xf * dexp_ref[:, lo:lo + LANES]
            zf = z_ref[:, lo:lo + LANES].astype(F32)
            act[:, lo:lo + LANES] = y * _silu(zf)
            xw[:, qq * LANES:(qq + 1) * LANES] = (xf * pair(wgt, h0)).astype(BF16)
            st[g, :, qq * LANES:(qq + 1) * LANES] = (
                st_g[:, qq * LANES:(qq + 1) * LANES] * e_pair[L - 1:L, :])
        upd = lax.dot_general(bg, xw[...], (((0,), (0,)), ((), ())), preferred_element_type=F32)
        st[g] = st[g] + upd
        gs = slice(g * GROUP_CH, (g + 1) * GROUP_CH)
        y_ref[:, gs] = _rms(act[:, gs], gn_ref[:, gs]).astype(y_ref.dtype)


def _ssd(zx, dt_raw, conv_w, conv_b, dt_bias, a_log, d_exp, gate_g, *, bsz, seqlen):
    nc = seqlen // CHUNK
    row = lambda b, c: b * nc + c
    full = lambda shape: pl.BlockSpec(shape, lambda b, c: (0,) * len(shape))
    x_blk = D_INNER // D_INNER
    bc_blk = (2 * D_INNER) // (CONV_DIM - D_INNER)
    return pl.pallas_call(
        _ssd_kernel,
        grid=(bsz, nc),
        in_specs=[pl.BlockSpec((CHUNK, D_INNER), lambda b, c: (row(b, c), 0)),
                  pl.BlockSpec((CHUNK, D_INNER), lambda b, c: (row(b, c), x_blk)),
                  pl.BlockSpec((CHUNK, CONV_DIM - D_INNER), lambda b, c: (row(b, c), bc_blk)),
                  pl.BlockSpec((CHUNK, LANES), lambda b, c: (row(b, c), 0)),
                  full((CONV_WIDTH, CONV_DIM)), full((1, CONV_DIM)), full((1, LANES)), full((1, LANES)),
                  full((1, D_INNER)), full((1, D_INNER))],
        out_specs=pl.BlockSpec((CHUNK, D_INNER), lambda b, c: (row(b, c), 0)),
        out_shape=jax.ShapeDtypeStruct((bsz * seqlen, D_INNER), BF16),
        scratch_shapes=[pltpu.VMEM((CHUNK + CONV_HALO, CONV_DIM), F32),
                        pltpu.VMEM((CHUNK, CONV_DIM), F32),
                        pltpu.VMEM((CHUNK, GROUP_CH), BF16),
                        pltpu.VMEM((SSM_GROUPS, SSM_STATE, GROUP_CH), F32)],
        compiler_params=_cparams(2),
        name="ssd",
    )(zx, zx, zx, dt_raw, conv_w, conv_b, dt_bias, a_log, d_exp, gate_g)


def _kv_kernel(x_ref, g_ref, wdkv_ref, lg_ref, wuk_ref, wuv_ref, wkr_ref, ck_ref, sk_ref,
               k_ref, v_ref):
    h = _rms(x_ref[...], g_ref[...]).astype(BF16)
    ckv = _rms(_dot(h, wdkv_ref[...]), lg_ref[...]).astype(BF16)
    kn = _dot(ckv, wuk_ref[...]).astype(BF16)
    v_ref[...] = _dot(ckv, wuv_ref[...]).astype(BF16)
    kk = _dot(h, wkr_ref[...])
    kr = (kk * ck_ref[...] + pltpu.roll(kk, QK_ROPE, 1) * sk_ref[...]).astype(BF16)
    for hd in range(MLA_HEADS):
        k_ref[:, hd * QK_PAD:hd * QK_PAD + QK_NOPE] = kn[:, hd * QK_NOPE:(hd + 1) * QK_NOPE]
        k_ref[:, hd * QK_PAD + QK_NOPE:(hd + 1) * QK_PAD] = kr


def _shared_kv(x, g, wdkv, lg, wuk, wuv, wkr2, ck, sk, *, seqlen, tm=512):
    m, k = x.shape
    nt = seqlen // tm
    full = lambda a: pl.BlockSpec(a.shape, lambda i: (0,) * a.ndim)
    tab = pl.BlockSpec((tm, LANES), lambda i: (i % nt, 0))
    return pl.pallas_call(
        _kv_kernel,
        grid=(m // tm,),
        in_specs=[pl.BlockSpec((tm, k), lambda i: (i, 0)), full(g), full(wdkv), full(lg), full(wuk),
                  full(wuv), full(wkr2), tab, tab],
        out_specs=[pl.BlockSpec((tm, MLA_HEADS * QK_PAD), lambda i: (i, 0)),
                   pl.BlockSpec((tm, MLA_HEADS * V_HEAD), lambda i: (i, 0))],
        out_shape=[jax.ShapeDtypeStruct((m, MLA_HEADS * QK_PAD), BF16),
                   jax.ShapeDtypeStruct((m, MLA_HEADS * V_HEAD), BF16)],
        compiler_params=_cparams(1),
        name="shared_kv",
    )(x, g, wdkv, lg, wuk, wuv, wkr2, ck, sk)


def _q_kernel(x_ref, g_ref, wdq_ref, lg_ref, wn_ref, wr_ref, ws_ref, ck_ref, sk_ref, q_ref):
    h = _rms(x_ref[...], g_ref[...]).astype(BF16)
    cq = _rms(_dot(h, wdq_ref[...]), lg_ref[...]).astype(BF16)
    qn = (_dot(cq, wn_ref[...]) * SCORE_SCALE).astype(BF16)
    qr = _dot(cq, wr_ref[...])
    qs = _dot(cq, ws_ref[...])
    ck = ck_ref[...]
    sk = sk_ref[...]
    for hd in range(MLA_HEADS):
        q_ref[:, hd * QK_PAD:hd * QK_PAD + QK_NOPE] = qn[:, hd * QK_NOPE:(hd + 1) * QK_NOPE]
        sl = slice(hd * LANES, (hd + 1) * LANES)
        q_ref[:, hd * QK_PAD + QK_NOPE:(hd + 1) * QK_PAD] = (
            (qr[:, sl] * ck + qs[:, sl] * sk) * SCORE_SCALE).astype(BF16)


def _q_side(x, g, wdq, lg, wn, wr, ws, ck, sk, *, seqlen, tm=512):
    m, k = x.shape
    nt = seqlen // tm
    full = lambda a: pl.BlockSpec(a.shape, lambda i: (0,) * a.ndim)
    tab = pl.BlockSpec((tm, LANES), lambda i: (i % nt, 0))
    return pl.pallas_call(
        _q_kernel,
        grid=(m // tm,),
        in_specs=[pl.BlockSpec((tm, k), lambda i: (i, 0)), full(g), full(wdq), full(lg), full(wn),
                  full(wr), full(ws), tab, tab],
        out_specs=pl.BlockSpec((tm, MLA_HEADS * QK_PAD), lambda i: (i, 0)),
        out_shape=jax.ShapeDtypeStruct((m, MLA_HEADS * QK_PAD), BF16),
        compiler_params=_cparams(1),
        name="q_side",
    )(x, g, wdq, lg, wn, wr, ws, ck, sk)


def _attn_kernel(q_ref, k_ref, v_ref, o_ref, m_scr, l_scr, acc_scr, s_scr):
    qi = pl.program_id(2)
    m_scr[...] = jnp.full(m_scr.shape, -jnp.inf, F32)
    l_scr[...] = jnp.zeros(l_scr.shape, F32)
    acc_scr[...] = jnp.zeros(acc_scr.shape, F32)

    def score(kc, slot):
        k0 = pl.multiple_of(kc * ATT_TK, ATT_TK)
        for hh in range(ATT_HEADS_PER_STEP):
            q = q_ref[:, hh * QK_PAD:(hh + 1) * QK_PAD]
            k = k_ref[pl.ds(k0, ATT_TK), hh * QK_PAD:(hh + 1) * QK_PAD]
            s_scr[slot, hh] = lax.dot_general(k, q, (((1,), (1,)), ((), ())), preferred_element_type=F32)

    def step(kc, slot, diagonal):
        k0 = pl.multiple_of(kc * ATT_TK, ATT_TK)
        for hh in range(ATT_HEADS_PER_STEP):
            v = v_ref[pl.ds(k0, ATT_TK), hh * V_HEAD:(hh + 1) * V_HEAD]
            s = s_scr[slot, hh]
            if diagonal:
                key = lax.broadcasted_iota(I32, (ATT_TK, ATT_TQ), 0)
                qry = lax.broadcasted_iota(I32, (ATT_TK, ATT_TQ), 1)
                s = jnp.where(key <= qry, s, -jnp.inf)
            m_prev = m_scr[hh]
            m_new = jnp.maximum(m_prev, jnp.max(s, axis=0, keepdims=True))
            alpha = jnp.exp2(m_prev - m_new)
            p = jnp.exp2(s - m_new)
            l_scr[hh] = alpha * l_scr[hh] + jnp.sum(p, axis=0, keepdims=True)
            pv = lax.dot_general(v, p.astype(BF16), (((0,), (0,)), ((), ())), preferred_element_type=F32)
            acc_scr[hh] = alpha * acc_scr[hh] + pv
            m_scr[hh] = m_new

    def pair(j, carry):
        c = 2 * j
        score(c + 1, 1)
        step(c, 0, False)
        score(c + 2, 0)
        step(c + 1, 1, False)
        return carry

    score(0, 0)
    lax.fori_loop(0, qi // 2, pair, 0)

    @pl.when(qi % 2 == 0)
    def _():
        step(qi, 0, True)

    @pl.when(qi % 2 == 1)
    def _():
        score(qi, 1)
        step(qi - 1, 0, False)
        step(qi, 1, True)

    for hh in range(ATT_HEADS_PER_STEP):
        o_ref[:, hh * V_HEAD:(hh + 1) * V_HEAD] = (acc_scr[hh] / l_scr[hh]).T.astype(o_ref.dtype)


def _attention(q, k, v, *, bsz, seqlen):
    nq = seqlen // ATT_TQ
    hps = ATT_HEADS_PER_STEP
    return pl.pallas_call(
        _attn_kernel,
        grid=(bsz, MLA_HEADS // hps, nq),
        in_specs=[pl.BlockSpec((ATT_TQ, hps * QK_PAD), lambda b, h, i: (b * nq + i, h)),
                  pl.BlockSpec((seqlen, hps * QK_PAD), lambda b, h, i: (b, h)),
                  pl.BlockSpec((seqlen, hps * V_HEAD), lambda b, h, i: (b, h))],
        out_specs=pl.BlockSpec((ATT_TQ, hps * V_HEAD), lambda b, h, i: (b * nq + i, h)),
        out_shape=jax.ShapeDtypeStruct((bsz * seqlen, MLA_HEADS * V_HEAD), BF16),
        scratch_shapes=[pltpu.VMEM((hps, 1, ATT_TQ), F32), pltpu.VMEM((hps, 1, ATT_TQ), F32),
                        pltpu.VMEM((hps, V_HEAD, ATT_TQ), F32),
                        pltpu.VMEM((2, hps, ATT_TK, ATT_TQ), F32)],
        compiler_params=_cparams(3),
        name="attention",
    )(q, k, v)


def _router_kernel(x_ref, g_ref, wr_ref, br_ref, h_ref, route_ref, cnt_ref, carry):
    tm = x_ref.shape[0]

    @pl.when(pl.program_id(0) == 0)
    def _():
        carry[...] = jnp.zeros(carry.shape, F32)

    h = _rms(x_ref[...], g_ref[...])
    _store_token_rows(h_ref, h)
    h_hi = h.astype(BF16)
    h_lo = (h - h_hi.astype(F32)).astype(BF16)
    hw = _dot(h_hi, wr_ref[...])
    logits = (hw[:, :ROUTE_LANES] + hw[:, ROUTE_LANES:] + _dot(h_lo, wr_ref[:, :ROUTE_LANES])) + br_ref[...]
    lane = lax.broadcasted_iota(I32, (tm, ROUTE_LANES), 1)
    neg = -jnp.inf

    def first_argmax(v, vmax):
        return jnp.min(jnp.where(v == vmax, lane, ROUTE_LANES), axis=-1, keepdims=True)

    gl = jnp.where(lane < N_GROUPS, logits, neg)
    gm = jnp.max(gl, axis=-1, keepdims=True)
    g_w = 1.0 / jnp.sum(jnp.exp(gl - gm), axis=-1, keepdims=True)
    gi = first_argmax(gl, gm)
    lo = EXPERT_LANE0 + gi * EXPERTS_PER_GROUP
    el = jnp.where((lane >= lo) & (lane < lo + EXPERTS_PER_GROUP), logits, neg)
    m1 = jnp.max(el, axis=-1, keepdims=True)
    es = jnp.sum(jnp.exp(el - m1), axis=-1, keepdims=True)
    i1 = first_argmax(el, m1)
    el2 = jnp.where(lane == i1, neg, el)
    m2 = jnp.max(el2, axis=-1, keepdims=True)
    i2 = first_argmax(el2, m2)
    p1 = 1.0 / es
    p2 = jnp.exp(m2 - m1) / es
    den = p1 + p2
    w1 = g_w * (p1 / den)
    w2 = g_w * (p2 / den)
    sel1 = lane == i1
    sel2 = lane == i2
    onehot = jnp.where(sel1 | sel2, 1.0, 0.0)
    r_i = lax.broadcasted_iota(I32, (tm, tm), 0)
    c_i = lax.broadcasted_iota(I32, (tm, tm), 1)
    before = jnp.where(r_i > c_i, 1.0, 0.0).astype(BF16)
    base = carry[0:1, :] + _dot(before, onehot.astype(BF16))
    r1 = jnp.sum(jnp.where(sel1, base, 0.0), axis=-1, keepdims=True)
    r2 = jnp.sum(jnp.where(sel2, base, 0.0), axis=-1, keepdims=True)
    carry[0:1, :] = carry[0:1, :] + jnp.sum(onehot, axis=0, keepdims=True)
    cnt_ref[...] = carry[...]
    cols = [(i1 - EXPERT_LANE0).astype(F32), (i2 - EXPERT_LANE0).astype(F32), r1, r2, w1, w2]
    route = jnp.zeros((tm, ROUTE_LANES), F32)
    for n, cval in enumerate(cols):
        route = jnp.where(lane == n, cval, route)
    route_ref[...] = route


def _router(x, g, wr, br, *, tm=512):
    m, k = x.shape
    return pl.pallas_call(
        _router_kernel,
        grid=(m // tm,),
        in_specs=[pl.BlockSpec((tm, k), lambda i: (i, 0)),
                  pl.BlockSpec((1, k), lambda i: (0, 0)),
                  pl.BlockSpec((k, 2 * ROUTE_LANES), lambda i: (0, 0)),
                  pl.BlockSpec((1, ROUTE_LANES), lambda i: (0, 0))],
        out_specs=[pl.BlockSpec((tm * TOKEN_PITCH, LANES), lambda i: (i, 0)),
                   pl.BlockSpec((tm, ROUTE_LANES), lambda i: (i, 0)),
                   pl.BlockSpec((8, ROUTE_LANES), lambda i: (0, 0))],
        out_shape=[jax.ShapeDtypeStruct((m * TOKEN_PITCH, LANES), U32),
                   jax.ShapeDtypeStruct((m, ROUTE_LANES), F32),
                   jax.ShapeDtypeStruct((8, ROUTE_LANES), F32)],
        scratch_shapes=[pltpu.VMEM((8, ROUTE_LANES), F32)],
        compiler_params=_cparams(1),
        name="router",
    )(x, g, wr, br)


def _pack_bf16_pair(lo, hi):
    lo_bits = pltpu.bitcast(lo.astype(BF16).astype(F32), U32) >> 16
    hi_bits = pltpu.bitcast(hi.astype(BF16).astype(F32), U32) & jnp.uint32(HIGH_HALF)
    return lo_bits | hi_bits


def _unpack_bf16_pair(w):
    return pltpu.bitcast(w << 16, F32), pltpu.bitcast(w & jnp.uint32(HIGH_HALF), F32)


def _store_token_rows(ref, v):
    n = v.shape[0]
    for c in range(TOKEN_WORDS):
        hi = c + TOKEN_WORDS
        ref[pl.ds(c, n, stride=TOKEN_PITCH), :] = _pack_bf16_pair(v[:, c * LANES:(c + 1) * LANES],
                                                                   v[:, hi * LANES:(hi + 1) * LANES])
    for c in range(TOKEN_WORDS, TOKEN_PITCH):
        ref[pl.ds(c, n, stride=TOKEN_PITCH), :] = jnp.zeros((n, LANES), U32)


def _store_token_pairs(ref, c0, lo, hi):
    n = lo.shape[0]
    for k in range(lo.shape[1] // LANES):
        ref[pl.ds(c0 + k, n, stride=TOKEN_PITCH), :] = _pack_bf16_pair(lo[:, k * LANES:(k + 1) * LANES],
                                                                       hi[:, k * LANES:(k + 1) * LANES])


def _load_token_chunks(ref, first_token, n, c):
    return _unpack_bf16_pair(ref[pl.ds(first_token * TOKEN_PITCH + c, n, stride=TOKEN_PITCH), :])


def _token_copy(src_hbm, t, dst, r, sem):
    return pltpu.make_async_copy(src_hbm.at[pl.ds(t * TOKEN_PITCH, TOKEN_WORDS), :],
                                 dst.at[pl.ds(r * TOKEN_PITCH, TOKEN_WORDS), :], sem)


def _gather_tokens(idx_ref, first, last, src_hbm, dst, sem):
    for r in range(first, last):
        _token_copy(src_hbm, idx_ref[0, 0, r], dst, r, sem).start(priority=r % 2)


def _gather_tokens_loop(idx_ref, n_tokens, src_hbm, dst, sem):
    def body(r, carry):
        _token_copy(src_hbm, idx_ref[0, 0, r], dst, r, sem).start()
        return carry

    lax.fori_loop(0, n_tokens, body, 0)


def _wait_tokens(src_hbm, n_tokens, dst, sem):
    rows = n_tokens * TOKEN_WORDS
    pltpu.make_async_copy(src_hbm.at[pl.ds(0, rows), :], dst.at[pl.ds(0, rows), :], sem).wait()


def _expert_kernel(be_ref, nu_ref, src_cur, src_nxt, h_hbm, wg_ref, wu_ref, wd_ref, o_ref,
                   xbuf, sem, xs, hid, wg_b, wu_b, wd_b):
    b = pl.program_id(0)
    n_used = nu_ref[0]
    slot = b % 2
    half = D_EXPERT // 2
    n_issue = 12
    per_issue = -(-EXPERT_BLOCK // n_issue)

    def issue(i):
        _gather_tokens(src_nxt, min(i * per_issue, EXPERT_BLOCK), min((i + 1) * per_issue, EXPERT_BLOCK),
                       h_hbm, xbuf.at[1 - slot], sem.at[1 - slot])

    @pl.when(b == 0)
    def _():
        _gather_tokens_loop(src_cur, EXPERT_BLOCK, h_hbm, xbuf.at[0], sem.at[0])

    @pl.when(b < n_used)
    def _():
        @pl.when((b == 0) | (be_ref[b] != be_ref[jnp.maximum(b - 1, 0)]))
        def _():
            wg_b[...] = wg_ref[0].astype(BF16)
            wu_b[...] = wu_ref[0].astype(BF16)
            wd_b[...] = wd_ref[0].astype(BF16)

        _wait_tokens(h_hbm, EXPERT_BLOCK, xbuf.at[slot], sem.at[slot])
        for c in range(TOKEN_WORDS):
            lo, hi = _load_token_chunks(xbuf.at[slot], 0, EXPERT_BLOCK, c)
            xs[:, c * LANES:(c + 1) * LANES] = lo.astype(BF16)
            xs[:, (c + TOKEN_WORDS) * LANES:(c + TOKEN_WORDS + 1) * LANES] = hi.astype(BF16)
        x = xs[...]
        g0 = _dot(x, wg_b[:, :half])
        issue(0)
        u0 = _dot(x, wu_b[:, :half])
        issue(1)
        g1 = _dot(x, wg_b[:, half:])
        issue(2)
        hid[:, :half] = (_silu(g0) * u0).astype(BF16)
        u1 = _dot(x, wu_b[:, half:])
        issue(3)
        hid[:, half:] = (_silu(g1) * u1).astype(BF16)
        hv = hid[...]
        n_out = D_MODEL // (2 * LANES)
        pending = None
        for j in range(n_out // 2):
            y_lo = _dot(hv, wd_b[:, j * 2 * LANES:(j + 1) * 2 * LANES])
            issue(4 + 2 * j)
            if pending is not None:
                _store_token_pairs(o_ref, *pending)
            jh = j + n_out // 2
            y_hi = _dot(hv, wd_b[:, jh * 2 * LANES:(jh + 1) * 2 * LANES])
            issue(5 + 2 * j)
            pending = (2 * j, y_lo, y_hi)
        _store_token_pairs(o_ref, *pending)
        for c in range(TOKEN_WORDS, TOKEN_PITCH):
            o_ref[pl.ds(c, EXPERT_BLOCK, stride=TOKEN_PITCH), :] = jnp.zeros((EXPERT_BLOCK, LANES), U32)

    @pl.when(b >= n_used)
    def _():
        @pl.when(b == n_used)
        def _():
            _wait_tokens(h_hbm, EXPERT_BLOCK, xbuf.at[slot], sem.at[slot])

        o_ref[...] = jnp.zeros(o_ref.shape, U32)


def _expert_ffn(block_expert, n_used, src, h, w_gate, w_up, w_down, *, layer):
    nb = block_expert.shape[0]
    d = D_MODEL
    blk_rows = EXPERT_BLOCK * TOKEN_PITCH
    nxt = lambda b, be, nu: (jnp.minimum(b + 1, nb - 1), 0, 0)
    wsel = lambda b, be, nu: (layer * N_EXPERTS + be[b], 0, 0)
    grid_spec = pltpu.PrefetchScalarGridSpec(
        num_scalar_prefetch=2,
        grid=(nb,),
        in_specs=[pl.BlockSpec((1, 1, EXPERT_BLOCK), lambda b, be, nu: (b, 0, 0), memory_space=pltpu.SMEM),
                  pl.BlockSpec((1, 1, EXPERT_BLOCK), nxt, memory_space=pltpu.SMEM),
                  pl.BlockSpec(memory_space=pl.ANY),
                  pl.BlockSpec((1, d, D_EXPERT), wsel),
                  pl.BlockSpec((1, d, D_EXPERT), wsel),
                  pl.BlockSpec((1, D_EXPERT, d), wsel)],
        out_specs=pl.BlockSpec((blk_rows, LANES), lambda b, be, nu: (b, 0)),
        scratch_shapes=[pltpu.VMEM((2, blk_rows, LANES), U32),
                        pltpu.SemaphoreType.DMA((2,)),
                        pltpu.VMEM((EXPERT_BLOCK, d), BF16),
                        pltpu.VMEM((EXPERT_BLOCK, D_EXPERT), BF16),
                        pltpu.VMEM((d, D_EXPERT), BF16),
                        pltpu.VMEM((d, D_EXPERT), BF16),
                        pltpu.VMEM((D_EXPERT, d), BF16)])
    return pl.pallas_call(
        _expert_kernel,
        grid_spec=grid_spec,
        out_shape=jax.ShapeDtypeStruct((nb * blk_rows, LANES), U32),
        compiler_params=_cparams(1, disable_bounds_checks=True),
        name="expert_ffn",
    )(block_expert, n_used, src, src, h, w_gate, w_up, w_down)


def _combine_kernel(pos_cur, pos_nxt, x_ref, route_ref, fg_ref, y_hbm, o_ref, ybuf, sem, *, apply_final_norm):
    i = pl.program_id(0)
    n = pl.num_programs(0)
    tm = x_ref.shape[0]
    slot = i % 2
    per_issue = TOP_K * tm // TOKEN_WORDS

    @pl.when(i == 0)
    def _():
        _gather_tokens_loop(pos_cur, TOP_K * tm, y_hbm, ybuf.at[0], sem.at[0])

    _wait_tokens(y_hbm, TOP_K * tm, ybuf.at[slot], sem.at[slot])
    r = route_ref[...]
    g0 = r[:, 4:5]
    g1 = r[:, 5:6]
    for c in range(TOKEN_WORDS):
        first = _load_token_chunks(ybuf.at[slot], 0, tm, c)
        second = _load_token_chunks(ybuf.at[slot], tm, tm, c)
        _gather_tokens(pos_nxt, c * per_issue, (c + 1) * per_issue, y_hbm, ybuf.at[1 - slot], sem.at[1 - slot])
        for half in range(2):
            cs_ = slice((c + half * TOKEN_WORDS) * LANES, (c + half * TOKEN_WORDS + 1) * LANES)
            o_ref[:, cs_] = x_ref[:, cs_] + (g0 * first[half] + g1 * second[half])
    if apply_final_norm:
        o_ref[...] = _rms(o_ref[...], fg_ref[...])

    @pl.when(i == n - 1)
    def _():
        _wait_tokens(y_hbm, TOP_K * tm, ybuf.at[1 - slot], sem.at[1 - slot])


def _combine(pos, x, route, yb, final_g, *, apply_final_norm, tm=256):
    m, d = x.shape
    nt = m // tm
    return pl.pallas_call(
        functools.partial(_combine_kernel, apply_final_norm=apply_final_norm),
        grid=(nt,),
        in_specs=[pl.BlockSpec((1, 1, TOP_K * tm), lambda i: (i, 0, 0), memory_space=pltpu.SMEM),
                  pl.BlockSpec((1, 1, TOP_K * tm), lambda i: (jnp.minimum(i + 1, nt - 1), 0, 0),
                               memory_space=pltpu.SMEM),
                  pl.BlockSpec((tm, d), lambda i: (i, 0)),
                  pl.BlockSpec((tm, ROUTE_LANES), lambda i: (i, 0)),
                  pl.BlockSpec((1, d), lambda i: (0, 0)),
                  pl.BlockSpec(memory_space=pl.ANY)],
        out_specs=pl.BlockSpec((tm, d), lambda i: (i, 0)),
        out_shape=jax.ShapeDtypeStruct((m, d), F32),
        scratch_shapes=[pltpu.VMEM((2, TOP_K * tm * TOKEN_PITCH, LANES), U32),
                        pltpu.SemaphoreType.DMA((2,))],
        compiler_params=_cparams(1, disable_bounds_checks=True),
        name="combine",
    )(pos, pos, x, route, final_g, yb)


def _hier_moe(x, g, wr, br, w_gate, w_up, w_down, final_g, *, layer, apply_final_norm, combine_tm=256):
    n_tok = x.shape[0]
    h, route, cnt = _router(x, g, wr, br)
    counts = cnt[0, EXPERT_LANE0:EXPERT_LANE0 + N_EXPERTS].astype(I32)
    nblk = (counts + EXPERT_BLOCK - 1) // EXPERT_BLOCK
    bend = jnp.cumsum(nblk)
    bstart = bend - nblk
    nb = n_tok * TOP_K // EXPERT_BLOCK + N_EXPERTS
    expert = route[:, 0:TOP_K].astype(I32)
    rank = route[:, TOP_K:2 * TOP_K].astype(I32)
    pos = bstart[expert] * EXPERT_BLOCK + rank
    tok = jnp.broadcast_to(jnp.arange(n_tok, dtype=I32)[:, None], pos.shape)
    src = jnp.zeros((nb * EXPERT_BLOCK,), I32).at[pos.reshape(-1)].set(tok.reshape(-1))
    block_expert = jnp.sum((jnp.arange(nb, dtype=I32)[:, None] >= bend[None, :]).astype(I32), axis=1)
    block_expert = jnp.minimum(block_expert, N_EXPERTS - 1)
    yb = _expert_ffn(block_expert, bend[-1:].astype(I32), src.reshape(nb, 1, EXPERT_BLOCK),
                     h, w_gate, w_up, w_down, layer=layer)
    pos_tiles = pos.reshape(n_tok // combine_tm, combine_tm, TOP_K).transpose(0, 2, 1)
    pos_tiles = pos_tiles.reshape(n_tok // combine_tm, 1, TOP_K * combine_tm)
    return _combine(pos_tiles, x, route, yb, final_g, apply_final_norm=apply_final_norm, tm=combine_tm)


def _rope_tables(seqlen):
    inv_freq = ROPE_THETA ** (-jnp.arange(0, QK_ROPE, 2, dtype=F32) / QK_ROPE)
    ang = jnp.arange(seqlen, dtype=F32)[:, None] * inv_freq[None, :]
    cos, sin = jnp.cos(ang), jnp.sin(ang)
    zero = jnp.zeros((seqlen, LANES - QK_ROPE), F32)
    return (jnp.concatenate([cos, cos, zero], axis=-1), jnp.concatenate([-sin, sin, zero], axis=-1))


def _swap_halves(w):
    half = w.shape[-1] // 2
    return jnp.concatenate([w[..., half:], w[..., :half]], axis=-1)


def _router_weights(rg_w, rg_b, re_w, re_b):
    d = rg_w.shape[0]
    w = jnp.concatenate([rg_w, re_w.transpose(1, 0, 2).reshape(d, N_EXPERTS)], axis=1)
    b = jnp.concatenate([rg_b, re_b.reshape(N_EXPERTS)])
    pad = ROUTE_LANES - w.shape[1]
    w = jnp.pad(w, ((0, 0), (0, pad)))
    w_hi = w.astype(BF16)
    w_lo = (w - w_hi.astype(F32)).astype(BF16)
    return jnp.concatenate([w_hi, w_lo], axis=1), jnp.pad(b, (0, pad)).reshape(1, ROUTE_LANES)


def _row(v, width=None):
    v = v.reshape(1, -1).astype(F32)
    if width is not None:
        v = jnp.pad(v, ((0, 0), (0, width - v.shape[1])))
    return v


def kernel(x, ssm_norm_g, ssm_in_w, ssm_conv_w, ssm_conv_b, ssm_dt_bias, ssm_A_log, ssm_D, ssm_gate_norm_g, ssm_out_w, kv_norm_g, kv_w_dkv, kv_latent_g, kv_w_uk, kv_w_uv, kv_w_kr, attn_norm_g, q_w_dq, q_latent_g, q_w_uq, attn_w_o, ffn_norm_g, router_group_w, router_group_b, router_expert_w, router_expert_b, expert_w_gate, expert_w_up, expert_w_down, final_norm_g):
    bsz, seqlen, d = x.shape
    n_tok = bsz * seqlen
    xs = x.reshape(n_tok, d)
    n_ssm = ssm_in_w.shape[0]
    depth = ffn_norm_g.shape[0]
    ck, sk = _rope_tables(seqlen)
    zx_cols = D_INNER + CONV_DIM
    k_all = v_all = None
    w_gate_all = expert_w_gate.reshape(depth * N_EXPERTS, d, D_EXPERT)
    w_up_all = expert_w_up.reshape(depth * N_EXPERTS, d, D_EXPERT)
    w_down_all = expert_w_down.reshape(depth * N_EXPERTS, D_EXPERT, d)

    for layer in range(depth):
        if layer < n_ssm:
            i = layer
            w_in = ssm_in_w[i].astype(BF16)
            w_dt = jnp.pad(w_in[:, zx_cols:], ((0, 0), (0, LANES - SSM_HEADS)))
            zx, dt_raw = _in_proj(xs, _row(ssm_norm_g[i]), w_in, w_dt)
            yn = _ssd(zx, dt_raw, ssm_conv_w[i], _row(ssm_conv_b[i]), _row(ssm_dt_bias[i], LANES),
                      _row(ssm_A_log[i], LANES), _row(jnp.repeat(ssm_D[i], SSM_HEADDIM)),
                      _row(ssm_gate_norm_g[i]), bsz=bsz, seqlen=seqlen)
            xs = _mm_res(yn, ssm_out_w[i].astype(BF16), xs)
        else:
            j = layer - n_ssm
            if k_all is None:
                wkr2 = jnp.concatenate([kv_w_kr, _swap_halves(kv_w_kr)], axis=1).astype(BF16)
                k_all, v_all = _shared_kv(xs, _row(kv_norm_g), kv_w_dkv.astype(BF16), _row(kv_latent_g),
                                          kv_w_uk.astype(BF16), kv_w_uv.astype(BF16), wkr2, ck, sk,
                                          seqlen=seqlen)
            wq = q_w_uq[j].reshape(Q_LORA, MLA_HEADS, QK_NOPE + QK_ROPE)
            wn = wq[:, :, :QK_NOPE].reshape(Q_LORA, MLA_HEADS * QK_NOPE).astype(BF16)
            wr = wq[:, :, QK_NOPE:]
            lane_pad = ((0, 0), (0, 0), (0, LANES - QK_ROPE))
            wr_p = jnp.pad(wr, lane_pad).reshape(Q_LORA, MLA_HEADS * LANES).astype(BF16)
            ws_p = jnp.pad(_swap_halves(wr), lane_pad).reshape(Q_LORA, MLA_HEADS * LANES).astype(BF16)
            q_all = _q_side(xs, _row(attn_norm_g[j]), q_w_dq[j].astype(BF16), _row(q_latent_g[j]),
                            wn, wr_p, ws_p, ck, sk, seqlen=seqlen)
            o = _attention(q_all, k_all, v_all, bsz=bsz, seqlen=seqlen)
            xs = _mm_res(o, attn_w_o[j].astype(BF16), xs)
        wr_l, br_l = _router_weights(router_group_w[layer], router_group_b[layer],
                                     router_expert_w[layer], router_expert_b[layer])
        xs = _hier_moe(xs, _row(ffn_norm_g[layer]), wr_l, br_l, w_gate_all, w_up_all, w_down_all,
                       _row(final_norm_g), layer=layer, apply_final_norm=(layer == depth - 1))
    return xs.reshape(bsz, seqlen, d)
```

```python
import functools
import math

import jax
import jax.numpy as jnp
from jax import lax
from jax.experimental import pallas as pl
from jax.experimental.pallas import tpu as pltpu

F32 = jnp.float32
BF16 = jnp.bfloat16
I32 = jnp.int32
U32 = jnp.uint32
HIGHEST = lax.Precision.HIGHEST

EPS = 1e-6
D_MODEL = 2048

D_INNER = 4096
SSM_HEADDIM = 64
SSM_HEADS = 64
SSM_GROUPS = 8
SSM_STATE = 128
CONV_WIDTH = 4
CHUNK = 128
CONV_DIM = D_INNER + 2 * SSM_GROUPS * SSM_STATE
GROUP_CH = D_INNER // SSM_GROUPS
CONV_HALO = 8

MLA_HEADS = 16
Q_LORA = 512
KV_LORA = 512
QK_NOPE = 128
QK_ROPE = 64
V_HEAD = 128
ROPE_THETA = 10000.0
QK_PAD = 256
ATT_TQ = 512
ATT_TK = 512
ATT_HEADS_PER_STEP = 2
SCORE_SCALE = (QK_NOPE + QK_ROPE) ** -0.5 * math.log2(math.e)

N_GROUPS = 4
EXPERTS_PER_GROUP = 8
N_EXPERTS = 32
TOP_K = 2
D_EXPERT = 512
ROUTE_LANES = 128
EXPERT_LANE0 = N_GROUPS
EXPERT_BLOCK = 256
TOKEN_WORDS = D_MODEL // 256
TOKEN_PITCH = TOKEN_WORDS + 1
HIGH_HALF = 0xFFFF0000

LANES = 128
VMEM_LIMIT = 56 * 1024 * 1024


def _cparams(n_axes, **kw):
    return pltpu.CompilerParams(dimension_semantics=("arbitrary",) * n_axes,
                                vmem_limit_bytes=VMEM_LIMIT, **kw)


def _rms(xf, g):
    return xf * lax.rsqrt(jnp.mean(xf * xf, axis=-1, keepdims=True) + EPS) * g


def _dot(a, b):
    return jnp.dot(a, b, preferred_element_type=F32)


def _silu(v):
    half = 0.5 * v
    return half * jnp.tanh(half) + half


def _mm_res_kernel(a_ref, w_ref, r_ref, o_ref):
    o_ref[...] = r_ref[...] + _dot(a_ref[...], w_ref[...])


def _mm_res(a, w, res, *, tm=1024):
    m, k = a.shape
    n = w.shape[1]
    tm = min(tm, m)
    tn = min(n, 1024)
    return pl.pallas_call(
        _mm_res_kernel,
        grid=(m // tm, n // tn),
        in_specs=[pl.BlockSpec((tm, k), lambda i, j: (i, 0)),
                  pl.BlockSpec((k, tn), lambda i, j: (0, j)),
                  pl.BlockSpec((tm, tn), lambda i, j: (i, j))],
        out_specs=pl.BlockSpec((tm, tn), lambda i, j: (i, j)),
        out_shape=jax.ShapeDtypeStruct((m, n), F32),
        compiler_params=_cparams(2),
        name="mm_res",
    )(a, w, res)


def _in_proj_kernel(x_ref, g_ref, w_ref, wdt_ref, o_ref, odt_ref, h_scr):
    @pl.when(pl.program_id(1) == 0)
    def _():
        h = _rms(x_ref[...], g_ref[...]).astype(BF16)
        h_scr[...] = h
        odt_ref[...] = _dot(h, wdt_ref[...])

    o_ref[...] = _dot(h_scr[...], w_ref[...]).astype(o_ref.dtype)


def _in_proj(x, g, w, wdt, *, tm=1024, tn=2048):
    m, k = x.shape
    n = w.shape[1] // tn * tn
    tm = min(tm, m)
    return pl.pallas_call(
        _in_proj_kernel,
        grid=(m // tm, n // tn),
        in_specs=[pl.BlockSpec((tm, k), lambda i, j: (i, 0)),
                  pl.BlockSpec((1, k), lambda i, j: (0, 0)),
                  pl.BlockSpec((k, tn), lambda i, j: (0, j)),
                  pl.BlockSpec((k, LANES), lambda i, j: (0, 0))],
        out_specs=[pl.BlockSpec((tm, tn), lambda i, j: (i, j)),
                   pl.BlockSpec((tm, LANES), lambda i, j: (i, 0))],
        out_shape=[jax.ShapeDtypeStruct((m, n), BF16),
                   jax.ShapeDtypeStruct((m, LANES), F32)],
        scratch_shapes=[pltpu.VMEM((tm, k), BF16)],
        compiler_params=_cparams(2),
        name="in_proj",
    )(x, g, w, wdt)


def _softplus(v):
    return jnp.maximum(v, 0.0) + jnp.log(1.0 + jnp.exp(-jnp.abs(v)))


def _ssd_kernel(z_ref, x_ref, bc_ref, dt_ref, cw_ref, cb_ref, dtb_ref, alog_ref, dexp_ref, gn_ref,
                y_ref, ubuf, act, xw, st):
    L = CHUNK

    @pl.when(pl.program_id(1) == 0)
    def _():
        ubuf[0:CONV_HALO, :] = jnp.zeros((CONV_HALO, CONV_DIM), F32)
        st[...] = jnp.zeros(st.shape, F32)

    ubuf[CONV_HALO:CONV_HALO + L, 0:D_INNER] = x_ref[...].astype(F32)
    ubuf[CONV_HALO:CONV_HALO + L, D_INNER:CONV_DIM] = bc_ref[...].astype(F32)
    cblk = 512
    for j in range(CONV_DIM // cblk):
        cs_ = slice(j * cblk, (j + 1) * cblk)
        acc = cb_ref[:, cs_] + cw_ref[0:1, cs_] * ubuf[CONV_HALO - 3:CONV_HALO - 3 + L, cs_]
        for k in range(1, CONV_WIDTH):
            r0 = CONV_HALO - 3 + k
            acc = acc + cw_ref[k:k + 1, cs_] * ubuf[r0:r0 + L, cs_]
        act[:, cs_] = _silu(acc)
    ubuf[0:CONV_HALO, :] = ubuf[L:L + CONV_HALO, :]

    dtv = _softplus(dt_ref[...] + dtb_ref[...])
    a = dtv * (-jnp.exp(alog_ref[...]))
    row = lax.broadcasted_iota(I32, (L, L), 0)
    col = lax.broadcasted_iota(I32, (L, L), 1)
    causal = row >= col
    cs = jnp.dot(causal.astype(F32), a, precision=HIGHEST, preferred_element_type=F32)
    cs_t = cs.T
    dt_t = dtv.T
    ecs = jnp.exp(cs)
    wgt = dtv * jnp.exp(cs[L - 1:L, :] - cs)
    first_head = col < SSM_HEADDIM

    def pair(v, h0):
        return jnp.where(first_head, v[:, h0:h0 + 1], v[:, h0 + 1:h0 + 2])

    for g in range(SSM_GROUPS):
        b0 = D_INNER + g * SSM_STATE
        c0 = D_INNER + SSM_GROUPS * SSM_STATE + g * SSM_STATE
        bg = act[:, b0:b0 + SSM_STATE].astype(BF16)
        cg = act[:, c0:c0 + SSM_STATE].astype(BF16)
        cb = lax.dot_general(cg, bg, (((1,), (1,)), ((), ())), preferred_element_type=F32)
        st_g = st[g]
        y_off = _dot(cg, st_g.astype(BF16))
        for qq in range(GROUP_CH // LANES):
            lo = g * GROUP_CH + qq * LANES
            h0 = lo // SSM_HEADDIM
            xf = act[:, lo:lo + LANES]
            xb = xf.astype(BF16)

            def head_mat(h):
                dec = jnp.exp(jnp.where(causal, cs[:, h:h + 1] - cs_t[h:h + 1, :], -jnp.inf))
                return (cb * dec * dt_t[h:h + 1, :]).astype(BF16)

            y_diag = jnp.where(first_head, _dot(head_mat(h0), xb), _dot(head_mat(h0 + 1), xb))
            e_pair = pair(ecs, h0)
            y = y_diag + y_off[:, qq * LANES:(qq + 1) * LANES] * e_pair + xf * dexp_ref[:, lo:lo + LANES]
            zf = z_ref[:, lo:lo + LANES].astype(F32)
            act[:, lo:lo + LANES] = y * _silu(zf)
            xw[:, qq * LANES:(qq + 1) * LANES] = (xf * pair(wgt, h0)).astype(BF16)
            st[g, :, qq * LANES:(qq + 1) * LANES] = (
                st_g[:, qq * LANES:(qq + 1) * LANES] * e_pair[L - 1:L, :])
        upd = lax.dot_general(bg, xw[...], (((0,), (0,)), ((), ())), preferred_element_type=F32)
        st[g] = st[g] + upd
        gs = slice(g * GROUP_CH, (g + 1) * GROUP_CH)
        y_ref[:, gs] = _rms(act[:, gs], gn_ref[:, gs]).astype(y_ref.dtype)


def _ssd(zx, dt_raw, conv_w, conv_b, dt_bias, a_log, d_exp, gate_g, *, bsz, seqlen):
    nc = seqlen // CHUNK
    row = lambda b, c: b * nc + c
    full = lambda shape: pl.BlockSpec(shape, lambda b, c: (0,) * len(shape))
    x_blk = D_INNER // D_INNER
    bc_blk = (2 * D_INNER) // (CONV_DIM - D_INNER)
    return pl.pallas_call(
        _ssd_kernel,
        grid=(bsz, nc),
        in_specs=[pl.BlockSpec((CHUNK, D_INNER), lambda b, c: (row(b, c), 0)),
                  pl.BlockSpec((CHUNK, D_INNER), lambda b, c: (row(b, c), x_blk)),
                  pl.BlockSpec((CHUNK, CONV_DIM - D_INNER), lambda b, c: (row(b, c), bc_blk)),
                  pl.BlockSpec((CHUNK, LANES), lambda b, c: (row(b, c), 0)),
                  full((CONV_WIDTH, CONV_DIM)), full((1, CONV_DIM)), full((1, LANES)), full((1, LANES)),
                  full((1, D_INNER)), full((1, D_INNER))],
        out_specs=pl.BlockSpec((CHUNK, D_INNER), lambda b, c: (row(b, c), 0)),
        out_shape=jax.ShapeDtypeStruct((bsz * seqlen, D_INNER), BF16),
        scratch_shapes=[pltpu.VMEM((CHUNK + CONV_HALO, CONV_DIM), F32),
                        pltpu.VMEM((CHUNK, CONV_DIM), F32),
                        pltpu.VMEM((CHUNK, GROUP_CH), BF16),
                        pltpu.VMEM((SSM_GROUPS, SSM_STATE, GROUP_CH), F32)],
        compiler_params=_cparams(2),
        name="ssd",
    )(zx, zx, zx, dt_raw, conv_w, conv_b, dt_bias, a_log, d_exp, gate_g)


def _kv_kernel(x_ref, g_ref, wdkv_ref, lg_ref, wuk_ref, wuv_ref, wkr_ref, ck_ref, sk_ref,
               k_ref, v_ref):
    h = _rms(x_ref[...], g_ref[...]).astype(BF16)
    ckv = _rms(_dot(h, wdkv_ref[...]), lg_ref[...]).astype(BF16)
    kn = _dot(ckv, wuk_ref[...]).astype(BF16)
    v_ref[...] = _dot(ckv, wuv_ref[...]).astype(BF16)
    kk = _dot(h, wkr_ref[...])
    kr = (kk * ck_ref[...] + pltpu.roll(kk, QK_ROPE, 1) * sk_ref[...]).astype(BF16)
    for hd in range(MLA_HEADS):
        k_ref[:, hd * QK_PAD:hd * QK_PAD + QK_NOPE] = kn[:, hd * QK_NOPE:(hd + 1) * QK_NOPE]
        k_ref[:, hd * QK_PAD + QK_NOPE:(hd + 1) * QK_PAD] = kr


def _shared_kv(x, g, wdkv, lg, wuk, wuv, wkr2, ck, sk, *, seqlen, tm=512):
    m, k = x.shape
    nt = seqlen // tm
    full = lambda a: pl.BlockSpec(a.shape, lambda i: (0,) * a.ndim)
    tab = pl.BlockSpec((tm, LANES), lambda i: (i % nt, 0))
    return pl.pallas_call(
        _kv_kernel,
        grid=(m // tm,),
        in_specs=[pl.BlockSpec((tm, k), lambda i: (i, 0)), full(g), full(wdkv), full(lg), full(wuk),
                  full(wuv), full(wkr2), tab, tab],
        out_specs=[pl.BlockSpec((tm, MLA_HEADS * QK_PAD), lambda i: (i, 0)),
                   pl.BlockSpec((tm, MLA_HEADS * V_HEAD), lambda i: (i, 0))],
        out_shape=[jax.ShapeDtypeStruct((m, MLA_HEADS * QK_PAD), BF16),
                   jax.ShapeDtypeStruct((m, MLA_HEADS * V_HEAD), BF16)],
        compiler_params=_cparams(1),
        name="shared_kv",
    )(x, g, wdkv, lg, wuk, wuv, wkr2, ck, sk)


def _q_kernel(x_ref, g_ref, wdq_ref, lg_ref, wn_ref, wr_ref, ws_ref, ck_ref, sk_ref, q_ref):
    h = _rms(x_ref[...], g_ref[...]).astype(BF16)
    cq = _rms(_dot(h, wdq_ref[...]), lg_ref[...]).astype(BF16)
    qn = (_dot(cq, wn_ref[...]) * SCORE_SCALE).astype(BF16)
    qr = _dot(cq, wr_ref[...])
    qs = _dot(cq, ws_ref[...])
    ck = ck_ref[...]
    sk = sk_ref[...]
    for hd in range(MLA_HEADS):
        q_ref[:, hd * QK_PAD:hd * QK_PAD + QK_NOPE] = qn[:, hd * QK_NOPE:(hd + 1) * QK_NOPE]
        sl = slice(hd * LANES, (hd + 1) * LANES)
        q_ref[:, hd * QK_PAD + QK_NOPE:(hd + 1) * QK_PAD] = (
            (qr[:, sl] * ck + qs[:, sl] * sk) * SCORE_SCALE).astype(BF16)


def _q_side(x, g, wdq, lg, wn, wr, ws, ck, sk, *, seqlen, tm=512):
    m, k = x.shape
    nt = seqlen // tm
    full = lambda a: pl.BlockSpec(a.shape, lambda i: (0,) * a.ndim)
    tab = pl.BlockSpec((tm, LANES), lambda i: (i % nt, 0))
    return pl.pallas_call(
        _q_kernel,
        grid=(m // tm,),
        in_specs=[pl.BlockSpec((tm, k), lambda i: (i, 0)), full(g), full(wdq), full(lg), full(wn),
                  full(wr), full(ws), tab, tab],
        out_specs=pl.BlockSpec((tm, MLA_HEADS * QK_PAD), lambda i: (i, 0)),
        out_shape=jax.ShapeDtypeStruct((m, MLA_HEADS * QK_PAD), BF16),
        compiler_params=_cparams(1),
        name="q_side",
    )(x, g, wdq, lg, wn, wr, ws, ck, sk)


def _attn_kernel(q_ref, k_ref, v_ref, o_ref, m_scr, l_scr, acc_scr, s_scr):
    def query_tile(qi, carry):
        q0 = pl.multiple_of(qi * ATT_TQ, ATT_TQ)
        m_scr[...] = jnp.full(m_scr.shape, -jnp.inf, F32)
        l_scr[...] = jnp.zeros(l_scr.shape, F32)
        acc_scr[...] = jnp.zeros(acc_scr.shape, F32)

        def score(kc, slot):
            k0 = pl.multiple_of(kc * ATT_TK, ATT_TK)
            for hh in range(ATT_HEADS_PER_STEP):
                q = q_ref[pl.ds(q0, ATT_TQ), hh * QK_PAD:(hh + 1) * QK_PAD]
                k = k_ref[pl.ds(k0, ATT_TK), hh * QK_PAD:(hh + 1) * QK_PAD]
                s_scr[slot, hh] = lax.dot_general(k, q, (((1,), (1,)), ((), ())), preferred_element_type=F32)

        def step(kc, slot, diagonal):
            k0 = pl.multiple_of(kc * ATT_TK, ATT_TK)
            for hh in range(ATT_HEADS_PER_STEP):
                v = v_ref[pl.ds(k0, ATT_TK), hh * V_HEAD:(hh + 1) * V_HEAD]
                s = s_scr[slot, hh]
                if diagonal:
                    key = lax.broadcasted_iota(I32, (ATT_TK, ATT_TQ), 0)
                    qry = lax.broadcasted_iota(I32, (ATT_TK, ATT_TQ), 1)
                    s = jnp.where(key <= qry, s, -jnp.inf)
                m_prev = m_scr[hh]
                m_new = jnp.maximum(m_prev, jnp.max(s, axis=0, keepdims=True))
                alpha = jnp.exp2(m_prev - m_new)
                p = jnp.exp2(s - m_new)
                l_scr[hh] = alpha * l_scr[hh] + jnp.sum(p, axis=0, keepdims=True)
                pv = lax.dot_general(v, p.astype(BF16), (((0,), (0,)), ((), ())), preferred_element_type=F32)
                acc_scr[hh] = alpha * acc_scr[hh] + pv
                m_scr[hh] = m_new

        def pair(j, inner):
            c = 2 * j
            score(c + 1, 1)
            step(c, 0, False)
            score(c + 2, 0)
            step(c + 1, 1, False)
            return inner

        score(0, 0)
        lax.fori_loop(0, qi // 2, pair, 0)

        @pl.when(qi % 2 == 0)
        def _():
            step(qi, 0, True)

        @pl.when(qi % 2 == 1)
        def _():
            score(qi, 1)
            step(qi - 1, 0, False)
            step(qi, 1, True)

        for hh in range(ATT_HEADS_PER_STEP):
            o_ref[pl.ds(q0, ATT_TQ), hh * V_HEAD:(hh + 1) * V_HEAD] = (
                (acc_scr[hh] / l_scr[hh]).T.astype(o_ref.dtype))
        return carry

    lax.fori_loop(0, q_ref.shape[0] // ATT_TQ, query_tile, 0)


def _attention(q, k, v, *, bsz, seqlen):
    hps = ATT_HEADS_PER_STEP
    seq_block = lambda width: pl.BlockSpec((seqlen, hps * width), lambda b, h: (b, h))
    return pl.pallas_call(
        _attn_kernel,
        grid=(bsz, MLA_HEADS // hps),
        in_specs=[seq_block(QK_PAD), seq_block(QK_PAD), seq_block(V_HEAD)],
        out_specs=seq_block(V_HEAD),
        out_shape=jax.ShapeDtypeStruct((bsz * seqlen, MLA_HEADS * V_HEAD), BF16),
        scratch_shapes=[pltpu.VMEM((hps, 1, ATT_TQ), F32), pltpu.VMEM((hps, 1, ATT_TQ), F32),
                        pltpu.VMEM((hps, V_HEAD, ATT_TQ), F32),
                        pltpu.VMEM((2, hps, ATT_TK, ATT_TQ), F32)],
        compiler_params=_cparams(2),
        name="attention",
    )(q, k, v)


def _router_kernel(x_ref, g_ref, wr_ref, br_ref, h_ref, route_ref, cnt_ref, carry):
    tm = x_ref.shape[0]

    @pl.when(pl.program_id(0) == 0)
    def _():
        carry[...] = jnp.zeros(carry.shape, F32)

    h = _rms(x_ref[...], g_ref[...])
    _store_token_rows(h_ref, h)
    h_hi = h.astype(BF16)
    h_lo = (h - h_hi.astype(F32)).astype(BF16)
    hw = _dot(h_hi, wr_ref[...])
    logits = (hw[:, :ROUTE_LANES] + hw[:, ROUTE_LANES:] + _dot(h_lo, wr_ref[:, :ROUTE_LANES])) + br_ref[...]
    lane = lax.broadcasted_iota(I32, (tm, ROUTE_LANES), 1)
    neg = -jnp.inf

    def first_argmax(v, vmax):
        return jnp.min(jnp.where(v == vmax, lane, ROUTE_LANES), axis=-1, keepdims=True)

    gl = jnp.where(lane < N_GROUPS, logits, neg)
    gm = jnp.max(gl, axis=-1, keepdims=True)
    g_w = 1.0 / jnp.sum(jnp.exp(gl - gm), axis=-1, keepdims=True)
    gi = first_argmax(gl, gm)
    lo = EXPERT_LANE0 + gi * EXPERTS_PER_GROUP
    el = jnp.where((lane >= lo) & (lane < lo + EXPERTS_PER_GROUP), logits, neg)
    m1 = jnp.max(el, axis=-1, keepdims=True)
    es = jnp.sum(jnp.exp(el - m1), axis=-1, keepdims=True)
    i1 = first_argmax(el, m1)
    el2 = jnp.where(lane == i1, neg, el)
    m2 = jnp.max(el2, axis=-1, keepdims=True)
    i2 = first_argmax(el2, m2)
    p1 = 1.0 / es
    p2 = jnp.exp(m2 - m1) / es
    den = p1 + p2
    w1 = g_w * (p1 / den)
    w2 = g_w * (p2 / den)
    sel1 = lane == i1
    sel2 = lane == i2
    onehot = jnp.where(sel1 | sel2, 1.0, 0.0)
    r_i = lax.broadcasted_iota(I32, (tm, tm), 0)
    c_i = lax.broadcasted_iota(I32, (tm, tm), 1)
    before = jnp.where(r_i > c_i, 1.0, 0.0).astype(BF16)
    base = carry[0:1, :] + _dot(before, onehot.astype(BF16))
    r1 = jnp.sum(jnp.where(sel1, base, 0.0), axis=-1, keepdims=True)
    r2 = jnp.sum(jnp.where(sel2, base, 0.0), axis=-1, keepdims=True)
    carry[0:1, :] = carry[0:1, :] + jnp.sum(onehot, axis=0, keepdims=True)
    cnt_ref[...] = carry[...]
    cols = [(i1 - EXPERT_LANE0).astype(F32), (i2 - EXPERT_LANE0).astype(F32), r1, r2, w1, w2]
    route = jnp.zeros((tm, ROUTE_LANES), F32)
    for n, cval in enumerate(cols):
        route = jnp.where(lane == n, cval, route)
    route_ref[...] = route


def _router(x, g, wr, br, *, tm=512):
    m, k = x.shape
    return pl.pallas_call(
        _router_kernel,
        grid=(m // tm,),
        in_specs=[pl.BlockSpec((tm, k), lambda i: (i, 0)),
                  pl.BlockSpec((1, k), lambda i: (0, 0)),
                  pl.BlockSpec((k, 2 * ROUTE_LANES), lambda i: (0, 0)),
                  pl.BlockSpec((1, ROUTE_LANES), lambda i: (0, 0))],
        out_specs=[pl.BlockSpec((tm * TOKEN_PITCH, LANES), lambda i: (i, 0)),
                   pl.BlockSpec((tm, ROUTE_LANES), lambda i: (i, 0)),
                   pl.BlockSpec((8, ROUTE_LANES), lambda i: (0, 0))],
        out_shape=[jax.ShapeDtypeStruct((m * TOKEN_PITCH, LANES), U32),
                   jax.ShapeDtypeStruct((m, ROUTE_LANES), F32),
                   jax.ShapeDtypeStruct((8, ROUTE_LANES), F32)],
        scratch_shapes=[pltpu.VMEM((8, ROUTE_LANES), F32)],
        compiler_params=_cparams(1),
        name="router",
    )(x, g, wr, br)


def _pack_bf16_pair(lo, hi):
    lo_bits = pltpu.bitcast(lo.astype(BF16).astype(F32), U32) >> 16
    hi_bits = pltpu.bitcast(hi.astype(BF16).astype(F32), U32) & jnp.uint32(HIGH_HALF)
    return lo_bits | hi_bits


def _unpack_bf16_pair(w):
    return pltpu.bitcast(w << 16, F32), pltpu.bitcast(w & jnp.uint32(HIGH_HALF), F32)


def _store_token_rows(ref, v):
    n = v.shape[0]
    for c in range(TOKEN_WORDS):
        hi = c + TOKEN_WORDS
        ref[pl.ds(c, n, stride=TOKEN_PITCH), :] = _pack_bf16_pair(v[:, c * LANES:(c + 1) * LANES],
                                                                   v[:, hi * LANES:(hi + 1) * LANES])
    for c in range(TOKEN_WORDS, TOKEN_PITCH):
        ref[pl.ds(c, n, stride=TOKEN_PITCH), :] = jnp.zeros((n, LANES), U32)


def _store_token_pairs(ref, c0, lo, hi):
    n = lo.shape[0]
    for k in range(lo.shape[1] // LANES):
        ref[pl.ds(c0 + k, n, stride=TOKEN_PITCH), :] = _pack_bf16_pair(lo[:, k * LANES:(k + 1) * LANES],
                                                                       hi[:, k * LANES:(k + 1) * LANES])


def _load_token_chunks(ref, first_token, n, c):
    return _unpack_bf16_pair(ref[pl.ds(first_token * TOKEN_PITCH + c, n, stride=TOKEN_PITCH), :])


def _token_copy(src_hbm, t, dst, r, sem):
    return pltpu.make_async_copy(src_hbm.at[pl.ds(t * TOKEN_PITCH, TOKEN_WORDS), :],
                                 dst.at[pl.ds(r * TOKEN_PITCH, TOKEN_WORDS), :], sem)


def _gather_tokens(idx_ref, first, last, src_hbm, dst, sem):
    for r in range(first, last):
        _token_copy(src_hbm, idx_ref[0, 0, r], dst, r, sem).start(priority=r % 2)


def _gather_tokens_loop(idx_ref, n_tokens, src_hbm, dst, sem):
    def body(r, carry):
        _token_copy(src_hbm, idx_ref[0, 0, r], dst, r, sem).start()
        return carry

    lax.fori_loop(0, n_tokens, body, 0)


def _wait_tokens(src_hbm, n_tokens, dst, sem):
    rows = n_tokens * TOKEN_WORDS
    pltpu.make_async_copy(src_hbm.at[pl.ds(0, rows), :], dst.at[pl.ds(0, rows), :], sem).wait()


def _expert_kernel(be_ref, nu_ref, src_cur, src_nxt, h_hbm, wg_ref, wu_ref, wd_ref, o_ref,
                   xbuf, sem, xs, hid, wg_b, wu_b, wd_b):
    b = pl.program_id(0)
    n_used = nu_ref[0]
    slot = b % 2
    half = D_EXPERT // 2
    n_issue = 12
    per_issue = -(-EXPERT_BLOCK // n_issue)

    def issue(i):
        _gather_tokens(src_nxt, min(i * per_issue, EXPERT_BLOCK), min((i + 1) * per_issue, EXPERT_BLOCK),
                       h_hbm, xbuf.at[1 - slot], sem.at[1 - slot])

    @pl.when(b == 0)
    def _():
        _gather_tokens_loop(src_cur, EXPERT_BLOCK, h_hbm, xbuf.at[0], sem.at[0])

    @pl.when(b < n_used)
    def _():
        @pl.when((b == 0) | (be_ref[b] != be_ref[jnp.maximum(b - 1, 0)]))
        def _():
            wg_b[...] = wg_ref[0].astype(BF16)
            wu_b[...] = wu_ref[0].astype(BF16)
            wd_b[...] = wd_ref[0].astype(BF16)

        _wait_tokens(h_hbm, EXPERT_BLOCK, xbuf.at[slot], sem.at[slot])
        for c in range(TOKEN_WORDS):
            lo, hi = _load_token_chunks(xbuf.at[slot], 0, EXPERT_BLOCK, c)
            xs[:, c * LANES:(c + 1) * LANES] = lo.astype(BF16)
            xs[:, (c + TOKEN_WORDS) * LANES:(c + TOKEN_WORDS + 1) * LANES] = hi.astype(BF16)
        x = xs[...]
        g0 = _dot(x, wg_b[:, :half])
        issue(0)
        u0 = _dot(x, wu_b[:, :half])
        issue(1)
        g1 = _dot(x, wg_b[:, half:])
        issue(2)
        hid[:, :half] = (_silu(g0) * u0).astype(BF16)
        u1 = _dot(x, wu_b[:, half:])
        issue(3)
        hid[:, half:] = (_silu(g1) * u1).astype(BF16)
        hv = hid[...]
        n_out = D_MODEL // (2 * LANES)
        pending = None
        for j in range(n_out // 2):
            y_lo = _dot(hv, wd_b[:, j * 2 * LANES:(j + 1) * 2 * LANES])
            issue(4 + 2 * j)
            if pending is not None:
                _store_token_pairs(o_ref, *pending)
            jh = j + n_out // 2
            y_hi = _dot(hv, wd_b[:, jh * 2 * LANES:(jh + 1) * 2 * LANES])
            issue(5 + 2 * j)
            pending = (2 * j, y_lo, y_hi)
        _store_token_pairs(o_ref, *pending)
        for c in range(TOKEN_WORDS, TOKEN_PITCH):
            o_ref[pl.ds(c, EXPERT_BLOCK, stride=TOKEN_PITCH), :] = jnp.zeros((EXPERT_BLOCK, LANES), U32)

    @pl.when(b >= n_used)
    def _():
        @pl.when(b == n_used)
        def _():
            _wait_tokens(h_hbm, EXPERT_BLOCK, xbuf.at[slot], sem.at[slot])

        o_ref[...] = jnp.zeros(o_ref.shape, U32)


def _expert_ffn(block_expert, n_used, src, h, w_gate, w_up, w_down, *, layer):
    nb = block_expert.shape[0]
    d = D_MODEL
    blk_rows = EXPERT_BLOCK * TOKEN_PITCH
    nxt = lambda b, be, nu: (jnp.minimum(b + 1, nb - 1), 0, 0)
    wsel = lambda b, be, nu: (layer * N_EXPERTS + be[b], 0, 0)
    grid_spec = pltpu.PrefetchScalarGridSpec(
        num_scalar_prefetch=2,
        grid=(nb,),
        in_specs=[pl.BlockSpec((1, 1, EXPERT_BLOCK), lambda b, be, nu: (b, 0, 0), memory_space=pltpu.SMEM),
                  pl.BlockSpec((1, 1, EXPERT_BLOCK), nxt, memory_space=pltpu.SMEM),
                  pl.BlockSpec(memory_space=pl.ANY),
                  pl.BlockSpec((1, d, D_EXPERT), wsel),
                  pl.BlockSpec((1, d, D_EXPERT), wsel),
                  pl.BlockSpec((1, D_EXPERT, d), wsel)],
        out_specs=pl.BlockSpec((blk_rows, LANES), lambda b, be, nu: (b, 0)),
        scratch_shapes=[pltpu.VMEM((2, blk_rows, LANES), U32),
                        pltpu.SemaphoreType.DMA((2,)),
                        pltpu.VMEM((EXPERT_BLOCK, d), BF16),
                        pltpu.VMEM((EXPERT_BLOCK, D_EXPERT), BF16),
                        pltpu.VMEM((d, D_EXPERT), BF16),
                        pltpu.VMEM((d, D_EXPERT), BF16),
                        pltpu.VMEM((D_EXPERT, d), BF16)])
    return pl.pallas_call(
        _expert_kernel,
        grid_spec=grid_spec,
        out_shape=jax.ShapeDtypeStruct((nb * blk_rows, LANES), U32),
        compiler_params=_cparams(1, disable_bounds_checks=True),
        name="expert_ffn",
    )(block_expert, n_used, src, src, h, w_gate, w_up, w_down)


def _combine_kernel(pos_cur, pos_nxt, x_ref, route_ref, fg_ref, y_hbm, o_ref, ybuf, sem, *, apply_final_norm):
    i = pl.program_id(0)
    n = pl.num_programs(0)
    tm = x_ref.shape[0]
    slot = i % 2
    per_issue = TOP_K * tm // TOKEN_WORDS

    @pl.when(i == 0)
    def _():
        _gather_tokens_loop(pos_cur, TOP_K * tm, y_hbm, ybuf.at[0], sem.at[0])

    _wait_tokens(y_hbm, TOP_K * tm, ybuf.at[slot], sem.at[slot])
    r = route_ref[...]
    g0 = r[:, 4:5]
    g1 = r[:, 5:6]
    for c in range(TOKEN_WORDS):
        first = _load_token_chunks(ybuf.at[slot], 0, tm, c)
        second = _load_token_chunks(ybuf.at[slot], tm, tm, c)
        _gather_tokens(pos_nxt, c * per_issue, (c + 1) * per_issue, y_hbm, ybuf.at[1 - slot], sem.at[1 - slot])
        for half in range(2):
            cs_ = slice((c + half * TOKEN_WORDS) * LANES, (c + half * TOKEN_WORDS + 1) * LANES)
            o_ref[:, cs_] = x_ref[:, cs_] + (g0 * first[half] + g1 * second[half])
    if apply_final_norm:
        o_ref[...] = _rms(o_ref[...], fg_ref[...])

    @pl.when(i == n - 1)
    def _():
        _wait_tokens(y_hbm, TOP_K * tm, ybuf.at[1 - slot], sem.at[1 - slot])


def _combine(pos, x, route, yb, final_g, *, apply_final_norm, tm=256):
    m, d = x.shape
    nt = m // tm
    return pl.pallas_call(
        functools.partial(_combine_kernel, apply_final_norm=apply_final_norm),
        grid=(nt,),
        in_specs=[pl.BlockSpec((1, 1, TOP_K * tm), lambda i: (i, 0, 0), memory_space=pltpu.SMEM),
                  pl.BlockSpec((1, 1, TOP_K * tm), lambda i: (jnp.minimum(i + 1, nt - 1), 0, 0),
                               memory_space=pltpu.SMEM),
                  pl.BlockSpec((tm, d), lambda i: (i, 0)),
                  pl.BlockSpec((tm, ROUTE_LANES), lambda i: (i, 0)),
                  pl.BlockSpec((1, d), lambda i: (0, 0)),
                  pl.BlockSpec(memory_space=pl.ANY)],
        out_specs=pl.BlockSpec((tm, d), lambda i: (i, 0)),
        out_shape=jax.ShapeDtypeStruct((m, d), F32),
        scratch_shapes=[pltpu.VMEM((2, TOP_K * tm * TOKEN_PITCH, LANES), U32),
                        pltpu.SemaphoreType.DMA((2,))],
        compiler_params=_cparams(1, disable_bounds_checks=True),
        name="combine",
    )(pos, pos, x, route, final_g, yb)


def _hier_moe(x, g, wr, br, w_gate, w_up, w_down, final_g, *, layer, apply_final_norm, combine_tm=256):
    n_tok = x.shape[0]
    h, route, cnt = _router(x, g, wr, br)
    counts = cnt[0, EXPERT_LANE0:EXPERT_LANE0 + N_EXPERTS].astype(I32)
    nblk = (counts + EXPERT_BLOCK - 1) // EXPERT_BLOCK
    bend = jnp.cumsum(nblk)
    bstart = bend - nblk
    nb = n_tok * TOP_K // EXPERT_BLOCK + N_EXPERTS
    expert = route[:, 0:TOP_K].astype(I32)
    rank = route[:, TOP_K:2 * TOP_K].astype(I32)
    pos = bstart[expert] * EXPERT_BLOCK + rank
    tok = jnp.broadcast_to(jnp.arange(n_tok, dtype=I32)[:, None], pos.shape)
    src = jnp.zeros((nb * EXPERT_BLOCK,), I32).at[pos.reshape(-1)].set(tok.reshape(-1))
    block_expert = jnp.sum((jnp.arange(nb, dtype=I32)[:, None] >= bend[None, :]).astype(I32), axis=1)
    block_expert = jnp.minimum(block_expert, N_EXPERTS - 1)
    yb = _expert_ffn(block_expert, bend[-1:].astype(I32), src.reshape(nb, 1, EXPERT_BLOCK),
                     h, w_gate, w_up, w_down, layer=layer)
    pos_tiles = pos.reshape(n_tok // combine_tm, combine_tm, TOP_K).transpose(0, 2, 1)
    pos_tiles = pos_tiles.reshape(n_tok // combine_tm, 1, TOP_K * combine_tm)
    return _combine(pos_tiles, x, route, yb, final_g, apply_final_norm=apply_final_norm, tm=combine_tm)


def _rope_tables(seqlen):
    inv_freq = ROPE_THETA ** (-jnp.arange(0, QK_ROPE, 2, dtype=F32) / QK_ROPE)
    ang = jnp.arange(seqlen, dtype=F32)[:, None] * inv_freq[None, :]
    cos, sin = jnp.cos(ang), jnp.sin(ang)
    zero = jnp.zeros((seqlen, LANES - QK_ROPE), F32)
    return (jnp.concatenate([cos, cos, zero], axis=-1), jnp.concatenate([-sin, sin, zero], axis=-1))


def _swap_halves(w):
    half = w.shape[-1] // 2
    return jnp.concatenate([w[..., half:], w[..., :half]], axis=-1)


def _router_weights(rg_w, rg_b, re_w, re_b):
    d = rg_w.shape[0]
    w = jnp.concatenate([rg_w, re_w.transpose(1, 0, 2).reshape(d, N_EXPERTS)], axis=1)
    b = jnp.concatenate([rg_b, re_b.reshape(N_EXPERTS)])
    pad = ROUTE_LANES - w.shape[1]
    w = jnp.pad(w, ((0, 0), (0, pad)))
    w_hi = w.astype(BF16)
    w_lo = (w - w_hi.astype(F32)).astype(BF16)
    return jnp.concatenate([w_hi, w_lo], axis=1), jnp.pad(b, (0, pad)).reshape(1, ROUTE_LANES)


def _row(v, width=None):
    v = v.reshape(1, -1).astype(F32)
    if width is not None:
        v = jnp.pad(v, ((0, 0), (0, width - v.shape[1])))
    return v


def kernel(x, ssm_norm_g, ssm_in_w, ssm_conv_w, ssm_conv_b, ssm_dt_bias, ssm_A_log, ssm_D, ssm_gate_norm_g, ssm_out_w, kv_norm_g, kv_w_dkv, kv_latent_g, kv_w_uk, kv_w_uv, kv_w_kr, attn_norm_g, q_w_dq, q_latent_g, q_w_uq, attn_w_o, ffn_norm_g, router_group_w, router_group_b, router_expert_w, router_expert_b, expert_w_gate, expert_w_up, expert_w_down, final_norm_g):
    bsz, seqlen, d = x.shape
    n_tok = bsz * seqlen
    xs = x.reshape(n_tok, d)
    n_ssm = ssm_in_w.shape[0]
    depth = ffn_norm_g.shape[0]
    ck, sk = _rope_tables(seqlen)
    zx_cols = D_INNER + CONV_DIM
    k_all = v_all = None
    w_gate_all = expert_w_gate.reshape(depth * N_EXPERTS, d, D_EXPERT)
    w_up_all = expert_w_up.reshape(depth * N_EXPERTS, d, D_EXPERT)
    w_down_all = expert_w_down.reshape(depth * N_EXPERTS, D_EXPERT, d)

    for layer in range(depth):
        if layer < n_ssm:
            i = layer
            w_in = ssm_in_w[i].astype(BF16)
            w_dt = jnp.pad(w_in[:, zx_cols:], ((0, 0), (0, LANES - SSM_HEADS)))
            zx, dt_raw = _in_proj(xs, _row(ssm_norm_g[i]), w_in, w_dt)
            yn = _ssd(zx, dt_raw, ssm_conv_w[i], _row(ssm_conv_b[i]), _row(ssm_dt_bias[i], LANES),
                      _row(ssm_A_log[i], LANES), _row(jnp.repeat(ssm_D[i], SSM_HEADDIM)),
                      _row(ssm_gate_norm_g[i]), bsz=bsz, seqlen=seqlen)
            xs = _mm_res(yn, ssm_out_w[i].astype(BF16), xs)
        else:
            j = layer - n_ssm
            if k_all is None:
                wkr2 = jnp.concatenate([kv_w_kr, _swap_halves(kv_w_kr)], axis=1).astype(BF16)
                k_all, v_all = _shared_kv(xs, _row(kv_norm_g), kv_w_dkv.astype(BF16), _row(kv_latent_g),
                                          kv_w_uk.astype(BF16), kv_w_uv.astype(BF16), wkr2, ck, sk,
                                          seqlen=seqlen)
            wq = q_w_uq[j].reshape(Q_LORA, MLA_HEADS, QK_NOPE + QK_ROPE)
            wn = wq[:, :, :QK_NOPE].reshape(Q_LORA, MLA_HEADS * QK_NOPE).astype(BF16)
            wr = wq[:, :, QK_NOPE:]
            lane_pad = ((0, 0), (0, 0), (0, LANES - QK_ROPE))
            wr_p = jnp.pad(wr, lane_pad).reshape(Q_LORA, MLA_HEADS * LANES).astype(BF16)
            ws_p = jnp.pad(_swap_halves(wr), lane_pad).reshape(Q_LORA, MLA_HEADS * LANES).astype(BF16)
            q_all = _q_side(xs, _row(attn_norm_g[j]), q_w_dq[j].astype(BF16), _row(q_latent_g[j]),
                            wn, wr_p, ws_p, ck, sk, seqlen=seqlen)
            o = _attention(q_all, k_all, v_all, bsz=bsz, seqlen=seqlen)
            xs = _mm_res(o, attn_w_o[j].astype(BF16), xs)
        wr_l, br_l = _router_weights(router_group_w[layer], router_group_b[layer],
                                     router_expert_w[layer], router_expert_b[layer])
        xs = _hier_moe(xs, _row(ffn_norm_g[layer]), wr_l, br_l, w_gate_all, w_up_all, w_down_all,
                       _row(final_norm_g), layer=layer, apply_final_norm=(layer == depth - 1))
    return xs.reshape(bsz, seqlen, d)
```

```python
import functools
import math

import jax
import jax.numpy as jnp
from jax import lax
from jax.experimental import pallas as pl
from jax.experimental.pallas import tpu as pltpu

F32 = jnp.float32
BF16 = jnp.bfloat16
I32 = jnp.int32
U32 = jnp.uint32
HIGHEST = lax.Precision.HIGHEST

EPS = 1e-6
D_MODEL = 2048

D_INNER = 4096
SSM_HEADDIM = 64
SSM_HEADS = 64
SSM_GROUPS = 8
SSM_STATE = 128
CONV_WIDTH = 4
CHUNK = 128
CONV_DIM = D_INNER + 2 * SSM_GROUPS * SSM_STATE
GROUP_CH = D_INNER // SSM_GROUPS
CONV_HALO = 8

MLA_HEADS = 16
Q_LORA = 512
KV_LORA = 512
QK_NOPE = 128
QK_ROPE = 64
V_HEAD = 128
ROPE_THETA = 10000.0
QK_PAD = 256
ATT_TQ = 512
ATT_TK = 512
ATT_HEADS_PER_STEP = 2
SCORE_SCALE = (QK_NOPE + QK_ROPE) ** -0.5 * math.log2(math.e)

N_GROUPS = 4
EXPERTS_PER_GROUP = 8
N_EXPERTS = 32
TOP_K = 2
D_EXPERT = 512
ROUTE_LANES = 128
EXPERT_LANE0 = N_GROUPS
EXPERT_BLOCK = 256
TOKEN_WORDS = D_MODEL // 256
TOKEN_PITCH = TOKEN_WORDS + 1
HIGH_HALF = 0xFFFF0000

LANES = 128
VMEM_LIMIT = 56 * 1024 * 1024


def _cparams(n_axes, **kw):
    return pltpu.CompilerParams(dimension_semantics=("arbitrary",) * n_axes,
                                vmem_limit_bytes=VMEM_LIMIT, **kw)


def _rms(xf, g):
    return xf * lax.rsqrt(jnp.mean(xf * xf, axis=-1, keepdims=True) + EPS) * g


def _dot(a, b):
    return jnp.dot(a, b, preferred_element_type=F32)


def _silu(v):
    half = 0.5 * v
    return half * jnp.tanh(half) + half


def _mm_res_kernel(a_ref, w_ref, r_ref, o_ref):
    o_ref[...] = r_ref[...] + _dot(a_ref[...], w_ref[...])


def _mm_res(a, w, res, *, tm=1024):
    m, k = a.shape
    n = w.shape[1]
    tm = min(tm, m)
    tn = min(n, 1024)
    return pl.pallas_call(
        _mm_res_kernel,
        grid=(m // tm, n // tn),
        in_specs=[pl.BlockSpec((tm, k), lambda i, j: (i, 0)),
                  pl.BlockSpec((k, tn), lambda i, j: (0, j)),
                  pl.BlockSpec((tm, tn), lambda i, j: (i, j))],
        out_specs=pl.BlockSpec((tm, tn), lambda i, j: (i, j)),
        out_shape=jax.ShapeDtypeStruct((m, n), F32),
        compiler_params=_cparams(2),
        name="mm_res",
    )(a, w, res)


def _in_proj_kernel(x_ref, g_ref, w_ref, wdt_ref, o_ref, odt_ref, h_scr):
    @pl.when(pl.program_id(1) == 0)
    def _():
        h = _rms(x_ref[...], g_ref[...]).astype(BF16)
        h_scr[...] = h
        odt_ref[...] = _dot(h, wdt_ref[...])

    o_ref[...] = _dot(h_scr[...], w_ref[0]).astype(o_ref.dtype)


def _in_proj(x, g, w, wdt, *, layer, tm=1024, tn=2048):
    m, k = x.shape
    n = w.shape[2] // tn * tn
    tm = min(tm, m)
    return pl.pallas_call(
        _in_proj_kernel,
        grid=(m // tm, n // tn),
        in_specs=[pl.BlockSpec((tm, k), lambda i, j: (i, 0)),
                  pl.BlockSpec((1, k), lambda i, j: (0, 0)),
                  pl.BlockSpec((1, k, tn), lambda i, j: (layer, 0, j)),
                  pl.BlockSpec((k, LANES), lambda i, j: (0, 0))],
        out_specs=[pl.BlockSpec((tm, tn), lambda i, j: (i, j)),
                   pl.BlockSpec((tm, LANES), lambda i, j: (i, 0))],
        out_shape=[jax.ShapeDtypeStruct((m, n), BF16),
                   jax.ShapeDtypeStruct((m, LANES), F32)],
        scratch_shapes=[pltpu.VMEM((tm, k), BF16)],
        compiler_params=_cparams(2),
        name="in_proj",
    )(x, g, w, wdt)


def _softplus(v):
    return jnp.maximum(v, 0.0) + jnp.log(1.0 + jnp.exp(-jnp.abs(v)))


def _ssd_kernel(z_ref, x_ref, bc_ref, dt_ref, cw_ref, cb_ref, dtb_ref, alog_ref, dexp_ref, gn_ref,
                y_ref, ubuf, act, xw, st):
    L = CHUNK

    @pl.when(pl.program_id(1) == 0)
    def _():
        ubuf[0:CONV_HALO, :] = jnp.zeros((CONV_HALO, CONV_DIM), F32)
        st[...] = jnp.zeros(st.shape, F32)

    ubuf[CONV_HALO:CONV_HALO + L, 0:D_INNER] = x_ref[...].astype(F32)
    ubuf[CONV_HALO:CONV_HALO + L, D_INNER:CONV_DIM] = bc_ref[...].astype(F32)
    cblk = 512
    for j in range(CONV_DIM // cblk):
        cs_ = slice(j * cblk, (j + 1) * cblk)
        acc = cb_ref[:, cs_] + cw_ref[0:1, cs_] * ubuf[CONV_HALO - 3:CONV_HALO - 3 + L, cs_]
        for k in range(1, CONV_WIDTH):
            r0 = CONV_HALO - 3 + k
            acc = acc + cw_ref[k:k + 1, cs_] * ubuf[r0:r0 + L, cs_]
        act[:, cs_] = _silu(acc)
    ubuf[0:CONV_HALO, :] = ubuf[L:L + CONV_HALO, :]

    dtv = _softplus(dt_ref[...] + dtb_ref[...])
    a = dtv * (-jnp.exp(alog_ref[...]))
    row = lax.broadcasted_iota(I32, (L, L), 0)
    col = lax.broadcasted_iota(I32, (L, L), 1)
    causal = row >= col
    cs = jnp.dot(causal.astype(F32), a, precision=HIGHEST, preferred_element_type=F32)
    cs_t = cs.T
    dt_t = dtv.T
    ecs = jnp.exp(cs)
    wgt = dtv * jnp.exp(cs[L - 1:L, :] - cs)
    first_head = col < SSM_HEADDIM

    def pair(v, h0):
        return jnp.where(first_head, v[:, h0:h0 + 1], v[:, h0 + 1:h0 + 2])

    for g in range(SSM_GROUPS):
        b0 = D_INNER + g * SSM_STATE
        c0 = D_INNER + SSM_GROUPS * SSM_STATE + g * SSM_STATE
        bg = act[:, b0:b0 + SSM_STATE].astype(BF16)
        cg = act[:, c0:c0 + SSM_STATE].astype(BF16)
        cb = lax.dot_general(cg, bg, (((1,), (1,)), ((), ())), preferred_element_type=F32)
        st_g = st[g]
        y_off = _dot(cg, st_g.astype(BF16))
        for qq in range(GROUP_CH // LANES):
            lo = g * GROUP_CH + qq * LANES
            h0 = lo // SSM_HEADDIM
            xf = act[:, lo:lo + LANES]
            xb = xf.astype(BF16)

            def head_mat(h):
                dec = jnp.exp(jnp.where(causal, cs[:, h:h + 1] - cs_t[h:h + 1, :], -jnp.inf))
                return (cb * dec * dt_t[h:h + 1, :]).astype(BF16)

            y_diag = jnp.where(first_head, _dot(head_mat(h0), xb), _dot(head_mat(h0 + 1), xb))
            e_pair = pair(ecs, h0)
            y = y_diag + y_off[:, qq * LANES:(qq + 1) * LANES] * e_pair + xf * dexp_ref[:, lo:lo + LANES]
            zf = z_ref[:, lo:lo + LANES].astype(F32)
            act[:, lo:lo + LANES] = y * _silu(zf)
            xw[:, qq * LANES:(qq + 1) * LANES] = (xf * pair(wgt, h0)).astype(BF16)
            st[g, :, qq * LANES:(qq + 1) * LANES] = (
                st_g[:, qq * LANES:(qq + 1) * LANES] * e_pair[L - 1:L, :])
        upd = lax.dot_general(bg, xw[...], (((0,), (0,)), ((), ())), preferred_element_type=F32)
        st[g] = st[g] + upd
        gs = slice(g * GROUP_CH, (g + 1) * GROUP_CH)
        y_ref[:, gs] = _rms(act[:, gs], gn_ref[:, gs]).astype(y_ref.dtype)


def _ssd(zx, dt_raw, conv_w, conv_b, dt_bias, a_log, d_exp, gate_g, *, bsz, seqlen):
    nc = seqlen // CHUNK
    row = lambda b, c: b * nc + c
    full = lambda shape: pl.BlockSpec(shape, lambda b, c: (0,) * len(shape))
    x_blk = D_INNER // D_INNER
    bc_blk = (2 * D_INNER) // (CONV_DIM - D_INNER)
    return pl.pallas_call(
        _ssd_kernel,
        grid=(bsz, nc),
        in_specs=[pl.BlockSpec((CHUNK, D_INNER), lambda b, c: (row(b, c), 0)),
                  pl.BlockSpec((CHUNK, D_INNER), lambda b, c: (row(b, c), x_blk)),
                  pl.BlockSpec((CHUNK, CONV_DIM - D_INNER), lambda b, c: (row(b, c), bc_blk)),
                  pl.BlockSpec((CHUNK, LANES), lambda b, c: (row(b, c), 0)),
                  full((CONV_WIDTH, CONV_DIM)), full((1, CONV_DIM)), full((1, LANES)), full((1, LANES)),
                  full((1, D_INNER)), full((1, D_INNER))],
        out_specs=pl.BlockSpec((CHUNK, D_INNER), lambda b, c: (row(b, c), 0)),
        out_shape=jax.ShapeDtypeStruct((bsz * seqlen, D_INNER), BF16),
        scratch_shapes=[pltpu.VMEM((CHUNK + CONV_HALO, CONV_DIM), F32),
                        pltpu.VMEM((CHUNK, CONV_DIM), F32),
                        pltpu.VMEM((CHUNK, GROUP_CH), BF16),
                        pltpu.VMEM((SSM_GROUPS, SSM_STATE, GROUP_CH), F32)],
        compiler_params=_cparams(2),
        name="ssd",
    )(zx, zx, zx, dt_raw, conv_w, conv_b, dt_bias, a_log, d_exp, gate_g)


def _kv_kernel(x_ref, g_ref, wdkv_ref, lg_ref, wuk_ref, wuv_ref, wkr_ref, ck_ref, sk_ref,
               k_ref, v_ref):
    h = _rms(x_ref[...], g_ref[...]).astype(BF16)
    ckv = _rms(_dot(h, wdkv_ref[...]), lg_ref[...]).astype(BF16)
    kn = _dot(ckv, wuk_ref[...]).astype(BF16)
    v_ref[...] = _dot(ckv, wuv_ref[...]).astype(BF16)
    kk = _dot(h, wkr_ref[...])
    kr = (kk * ck_ref[...] + pltpu.roll(kk, QK_ROPE, 1) * sk_ref[...]).astype(BF16)
    for hd in range(MLA_HEADS):
        k_ref[:, hd * QK_PAD:hd * QK_PAD + QK_NOPE] = kn[:, hd * QK_NOPE:(hd + 1) * QK_NOPE]
        k_ref[:, hd * QK_PAD + QK_NOPE:(hd + 1) * QK_PAD] = kr


def _shared_kv(x, g, wdkv, lg, wuk, wuv, wkr2, ck, sk, *, seqlen, tm=512):
    m, k = x.shape
    nt = seqlen // tm
    full = lambda a: pl.BlockSpec(a.shape, lambda i: (0,) * a.ndim)
    tab = pl.BlockSpec((tm, LANES), lambda i: (i % nt, 0))
    return pl.pallas_call(
        _kv_kernel,
        grid=(m // tm,),
        in_specs=[pl.BlockSpec((tm, k), lambda i: (i, 0)), full(g), full(wdkv), full(lg), full(wuk),
                  full(wuv), full(wkr2), tab, tab],
        out_specs=[pl.BlockSpec((tm, MLA_HEADS * QK_PAD), lambda i: (i, 0)),
                   pl.BlockSpec((tm, MLA_HEADS * V_HEAD), lambda i: (i, 0))],
        out_shape=[jax.ShapeDtypeStruct((m, MLA_HEADS * QK_PAD), BF16),
                   jax.ShapeDtypeStruct((m, MLA_HEADS * V_HEAD), BF16)],
        compiler_params=_cparams(1),
        name="shared_kv",
    )(x, g, wdkv, lg, wuk, wuv, wkr2, ck, sk)


def _q_kernel(x_ref, g_ref, wdq_ref, lg_ref, wn_ref, wr_ref, ws_ref, ck_ref, sk_ref, q_ref):
    h = _rms(x_ref[...], g_ref[...]).astype(BF16)
    cq = _rms(_dot(h, wdq_ref[...]), lg_ref[...]).astype(BF16)
    qn = (_dot(cq, wn_ref[...]) * SCORE_SCALE).astype(BF16)
    qr = _dot(cq, wr_ref[...])
    qs = _dot(cq, ws_ref[...])
    ck = ck_ref[...]
    sk = sk_ref[...]
    for hd in range(MLA_HEADS):
        q_ref[:, hd * QK_PAD:hd * QK_PAD + QK_NOPE] = qn[:, hd * QK_NOPE:(hd + 1) * QK_NOPE]
        sl = slice(hd * LANES, (hd + 1) * LANES)
        q_ref[:, hd * QK_PAD + QK_NOPE:(hd + 1) * QK_PAD] = (
            (qr[:, sl] * ck + qs[:, sl] * sk) * SCORE_SCALE).astype(BF16)


def _q_side(x, g, wdq, lg, wn, wr, ws, ck, sk, *, seqlen, tm=512):
    m, k = x.shape
    nt = seqlen // tm
    full = lambda a: pl.BlockSpec(a.shape, lambda i: (0,) * a.ndim)
    tab = pl.BlockSpec((tm, LANES), lambda i: (i % nt, 0))
    return pl.pallas_call(
        _q_kernel,
        grid=(m // tm,),
        in_specs=[pl.BlockSpec((tm, k), lambda i: (i, 0)), full(g), full(wdq), full(lg), full(wn),
                  full(wr), full(ws), tab, tab],
        out_specs=pl.BlockSpec((tm, MLA_HEADS * QK_PAD), lambda i: (i, 0)),
        out_shape=jax.ShapeDtypeStruct((m, MLA_HEADS * QK_PAD), BF16),
        compiler_params=_cparams(1),
        name="q_side",
    )(x, g, wdq, lg, wn, wr, ws, ck, sk)


def _attn_kernel(q_ref, k_ref, v_ref, o_ref, m_scr, l_scr, acc_scr, s_scr):
    def query_tile(qi, carry):
        q0 = pl.multiple_of(qi * ATT_TQ, ATT_TQ)
        m_scr[...] = jnp.full(m_scr.shape, -jnp.inf, F32)
        l_scr[...] = jnp.zeros(l_scr.shape, F32)
        acc_scr[...] = jnp.zeros(acc_scr.shape, F32)

        def score(kc, slot):
            k0 = pl.multiple_of(kc * ATT_TK, ATT_TK)
            for hh in range(ATT_HEADS_PER_STEP):
                q = q_ref[pl.ds(q0, ATT_TQ), hh * QK_PAD:(hh + 1) * QK_PAD]
                k = k_ref[pl.ds(k0, ATT_TK), hh * QK_PAD:(hh + 1) * QK_PAD]
                s_scr[slot, hh] = lax.dot_general(k, q, (((1,), (1,)), ((), ())), preferred_element_type=F32)

        def step(kc, slot, diagonal):
            k0 = pl.multiple_of(kc * ATT_TK, ATT_TK)
            for hh in range(ATT_HEADS_PER_STEP):
                v = v_ref[pl.ds(k0, ATT_TK), hh * V_HEAD:(hh + 1) * V_HEAD]
                s = s_scr[slot, hh]
                if diagonal:
                    key = lax.broadcasted_iota(I32, (ATT_TK, ATT_TQ), 0)
                    qry = lax.broadcasted_iota(I32, (ATT_TK, ATT_TQ), 1)
                    s = jnp.where(key <= qry, s, -jnp.inf)
                m_prev = m_scr[hh]
                m_new = jnp.maximum(m_prev, jnp.max(s, axis=0, keepdims=True))
                alpha = jnp.exp2(m_prev - m_new)
                p = jnp.exp2(s - m_new)
                l_scr[hh] = alpha * l_scr[hh] + jnp.sum(p, axis=0, keepdims=True)
                pv = lax.dot_general(v, p.astype(BF16), (((0,), (0,)), ((), ())), preferred_element_type=F32)
                acc_scr[hh] = alpha * acc_scr[hh] + pv
                m_scr[hh] = m_new

        def pair(j, inner):
            c = 2 * j
            score(c + 1, 1)
            step(c, 0, False)
            score(c + 2, 0)
            step(c + 1, 1, False)
            return inner

        score(0, 0)
        lax.fori_loop(0, qi // 2, pair, 0)

        @pl.when(qi % 2 == 0)
        def _():
            step(qi, 0, True)

        @pl.when(qi % 2 == 1)
        def _():
            score(qi, 1)
            step(qi - 1, 0, False)
            step(qi, 1, True)

        for hh in range(ATT_HEADS_PER_STEP):
            o_ref[pl.ds(q0, ATT_TQ), hh * V_HEAD:(hh + 1) * V_HEAD] = (
                (acc_scr[hh] / l_scr[hh]).T.astype(o_ref.dtype))
        return carry

    lax.fori_loop(0, q_ref.shape[0] // ATT_TQ, query_tile, 0)


def _attention(q, k, v, *, bsz, seqlen):
    hps = ATT_HEADS_PER_STEP
    seq_block = lambda width: pl.BlockSpec((seqlen, hps * width), lambda b, h: (b, h))
    return pl.pallas_call(
        _attn_kernel,
        grid=(bsz, MLA_HEADS // hps),
        in_specs=[seq_block(QK_PAD), seq_block(QK_PAD), seq_block(V_HEAD)],
        out_specs=seq_block(V_HEAD),
        out_shape=jax.ShapeDtypeStruct((bsz * seqlen, MLA_HEADS * V_HEAD), BF16),
        scratch_shapes=[pltpu.VMEM((hps, 1, ATT_TQ), F32), pltpu.VMEM((hps, 1, ATT_TQ), F32),
                        pltpu.VMEM((hps, V_HEAD, ATT_TQ), F32),
                        pltpu.VMEM((2, hps, ATT_TK, ATT_TQ), F32)],
        compiler_params=_cparams(2),
        name="attention",
    )(q, k, v)


def _router_kernel(x_ref, g_ref, wr_ref, br_ref, h_ref, route_ref, cnt_ref, carry):
    tm = x_ref.shape[0]

    @pl.when(pl.program_id(0) == 0)
    def _():
        carry[...] = jnp.zeros(carry.shape, F32)

    h = _rms(x_ref[...], g_ref[...])
    _store_token_rows(h_ref, h)
    h_hi = h.astype(BF16)
    h_lo = (h - h_hi.astype(F32)).astype(BF16)
    hw = _dot(h_hi, wr_ref[...])
    logits = (hw[:, :ROUTE_LANES] + hw[:, ROUTE_LANES:] + _dot(h_lo, wr_ref[:, :ROUTE_LANES])) + br_ref[...]
    lane = lax.broadcasted_iota(I32, (tm, ROUTE_LANES), 1)
    neg = -jnp.inf

    def first_argmax(v, vmax):
        return jnp.min(jnp.where(v == vmax, lane, ROUTE_LANES), axis=-1, keepdims=True)

    gl = jnp.where(lane < N_GROUPS, logits, neg)
    gm = jnp.max(gl, axis=-1, keepdims=True)
    g_w = 1.0 / jnp.sum(jnp.exp(gl - gm), axis=-1, keepdims=True)
    gi = first_argmax(gl, gm)
    lo = EXPERT_LANE0 + gi * EXPERTS_PER_GROUP
    el = jnp.where((lane >= lo) & (lane < lo + EXPERTS_PER_GROUP), logits, neg)
    m1 = jnp.max(el, axis=-1, keepdims=True)
    es = jnp.sum(jnp.exp(el - m1), axis=-1, keepdims=True)
    i1 = first_argmax(el, m1)
    el2 = jnp.where(lane == i1, neg, el)
    m2 = jnp.max(el2, axis=-1, keepdims=True)
    i2 = first_argmax(el2, m2)
    p1 = 1.0 / es
    p2 = jnp.exp(m2 - m1) / es
    den = p1 + p2
    w1 = g_w * (p1 / den)
    w2 = g_w * (p2 / den)
    sel1 = lane == i1
    sel2 = lane == i2
    onehot = jnp.where(sel1 | sel2, 1.0, 0.0)
    r_i = lax.broadcasted_iota(I32, (tm, tm), 0)
    c_i = lax.broadcasted_iota(I32, (tm, tm), 1)
    before = jnp.where(r_i > c_i, 1.0, 0.0).astype(BF16)
    base = carry[0:1, :] + _dot(before, onehot.astype(BF16))
    r1 = jnp.sum(jnp.where(sel1, base, 0.0), axis=-1, keepdims=True)
    r2 = jnp.sum(jnp.where(sel2, base, 0.0), axis=-1, keepdims=True)
    carry[0:1, :] = carry[0:1, :] + jnp.sum(onehot, axis=0, keepdims=True)
    cnt_ref[...] = carry[...]
    cols = [(i1 - EXPERT_LANE0).astype(F32), (i2 - EXPERT_LANE0).astype(F32), r1, r2, w1, w2]
    route = jnp.zeros((tm, ROUTE_LANES), F32)
    for n, cval in enumerate(cols):
        route = jnp.where(lane == n, cval, route)
    route_ref[...] = route


def _router(x, g, wr, br, *, tm=512):
    m, k = x.shape
    return pl.pallas_call(
        _router_kernel,
        grid=(m // tm,),
        in_specs=[pl.BlockSpec((tm, k), lambda i: (i, 0)),
                  pl.BlockSpec((1, k), lambda i: (0, 0)),
                  pl.BlockSpec((k, 2 * ROUTE_LANES), lambda i: (0, 0)),
                  pl.BlockSpec((1, ROUTE_LANES), lambda i: (0, 0))],
        out_specs=[pl.BlockSpec((tm * TOKEN_PITCH, LANES), lambda i: (i, 0)),
                   pl.BlockSpec((tm, ROUTE_LANES), lambda i: (i, 0)),
                   pl.BlockSpec((8, ROUTE_LANES), lambda i: (0, 0))],
        out_shape=[jax.ShapeDtypeStruct((m * TOKEN_PITCH, LANES), U32),
                   jax.ShapeDtypeStruct((m, ROUTE_LANES), F32),
                   jax.ShapeDtypeStruct((8, ROUTE_LANES), F32)],
        scratch_shapes=[pltpu.VMEM((8, ROUTE_LANES), F32)],
        compiler_params=_cparams(1),
        name="router",
    )(x, g, wr, br)


def _pack_bf16_pair(lo, hi):
    lo_bits = pltpu.bitcast(lo.astype(BF16).astype(F32), U32) >> 16
    hi_bits = pltpu.bitcast(hi.astype(BF16).astype(F32), U32) & jnp.uint32(HIGH_HALF)
    return lo_bits | hi_bits


def _unpack_bf16_pair(w):
    return pltpu.bitcast(w << 16, F32), pltpu.bitcast(w & jnp.uint32(HIGH_HALF), F32)


def _store_token_rows(ref, v):
    n = v.shape[0]
    for c in range(TOKEN_WORDS):
        hi = c + TOKEN_WORDS
        ref[pl.ds(c, n, stride=TOKEN_PITCH), :] = _pack_bf16_pair(v[:, c * LANES:(c + 1) * LANES],
                                                                   v[:, hi * LANES:(hi + 1) * LANES])
    for c in range(TOKEN_WORDS, TOKEN_PITCH):
        ref[pl.ds(c, n, stride=TOKEN_PITCH), :] = jnp.zeros((n, LANES), U32)


def _store_token_pairs(ref, c0, lo, hi):
    n = lo.shape[0]
    for k in range(lo.shape[1] // LANES):
        ref[pl.ds(c0 + k, n, stride=TOKEN_PITCH), :] = _pack_bf16_pair(lo[:, k * LANES:(k + 1) * LANES],
                                                                       hi[:, k * LANES:(k + 1) * LANES])


def _load_token_chunks(ref, first_token, n, c):
    return _unpack_bf16_pair(ref[pl.ds(first_token * TOKEN_PITCH + c, n, stride=TOKEN_PITCH), :])


def _token_copy(src_hbm, t, dst, r, sem):
    return pltpu.make_async_copy(src_hbm.at[pl.ds(t * TOKEN_PITCH, TOKEN_WORDS), :],
                                 dst.at[pl.ds(r * TOKEN_PITCH, TOKEN_WORDS), :], sem)


def _gather_tokens(idx_ref, first, last, src_hbm, dst, sem):
    for r in range(first, last):
        _token_copy(src_hbm, idx_ref[0, 0, r], dst, r, sem).start(priority=r % 2)


def _gather_tokens_loop(idx_ref, n_tokens, src_hbm, dst, sem):
    def body(r, carry):
        _token_copy(src_hbm, idx_ref[0, 0, r], dst, r, sem).start()
        return carry

    lax.fori_loop(0, n_tokens, body, 0)


def _wait_tokens(src_hbm, n_tokens, dst, sem):
    rows = n_tokens * TOKEN_WORDS
    pltpu.make_async_copy(src_hbm.at[pl.ds(0, rows), :], dst.at[pl.ds(0, rows), :], sem).wait()


def _expert_kernel(be_ref, nu_ref, src_cur, src_nxt, h_hbm, wg_ref, wu_ref, wd_ref, o_ref,
                   xbuf, sem, xs, hid, wg_b, wu_b, wd_b):
    b = pl.program_id(0)
    n_used = nu_ref[0]
    slot = b % 2
    half = D_EXPERT // 2
    n_issue = 12
    per_issue = -(-EXPERT_BLOCK // n_issue)

    def issue(i):
        _gather_tokens(src_nxt, min(i * per_issue, EXPERT_BLOCK), min((i + 1) * per_issue, EXPERT_BLOCK),
                       h_hbm, xbuf.at[1 - slot], sem.at[1 - slot])

    @pl.when(b == 0)
    def _():
        _gather_tokens_loop(src_cur, EXPERT_BLOCK, h_hbm, xbuf.at[0], sem.at[0])

    @pl.when(b < n_used)
    def _():
        @pl.when((b == 0) | (be_ref[b] != be_ref[jnp.maximum(b - 1, 0)]))
        def _():
            wg_b[...] = wg_ref[0].astype(BF16)
            wu_b[...] = wu_ref[0].astype(BF16)
            wd_b[...] = wd_ref[0].astype(BF16)

        _wait_tokens(h_hbm, EXPERT_BLOCK, xbuf.at[slot], sem.at[slot])
        for c in range(TOKEN_WORDS):
            lo, hi = _load_token_chunks(xbuf.at[slot], 0, EXPERT_BLOCK, c)
            xs[:, c * LANES:(c + 1) * LANES] = lo.astype(BF16)
            xs[:, (c + TOKEN_WORDS) * LANES:(c + TOKEN_WORDS + 1) * LANES] = hi.astype(BF16)
        x = xs[...]
        g0 = _dot(x, wg_b[:, :half])
        issue(0)
        u0 = _dot(x, wu_b[:, :half])
        issue(1)
        g1 = _dot(x, wg_b[:, half:])
        issue(2)
        hid[:, :half] = (_silu(g0) * u0).astype(BF16)
        u1 = _dot(x, wu_b[:, half:])
        issue(3)
        hid[:, half:] = (_silu(g1) * u1).astype(BF16)
        hv = hid[...]
        n_out = D_MODEL // (2 * LANES)
        pending = None
        for j in range(n_out // 2):
            y_lo = _dot(hv, wd_b[:, j * 2 * LANES:(j + 1) * 2 * LANES])
            issue(4 + 2 * j)
            if pending is not None:
                _store_token_pairs(o_ref, *pending)
            jh = j + n_out // 2
            y_hi = _dot(hv, wd_b[:, jh * 2 * LANES:(jh + 1) * 2 * LANES])
            issue(5 + 2 * j)
            pending = (2 * j, y_lo, y_hi)
        _store_token_pairs(o_ref, *pending)
        for c in range(TOKEN_WORDS, TOKEN_PITCH):
            o_ref[pl.ds(c, EXPERT_BLOCK, stride=TOKEN_PITCH), :] = jnp.zeros((EXPERT_BLOCK, LANES), U32)

    @pl.when(b >= n_used)
    def _():
        @pl.when(b == n_used)
        def _():
            _wait_tokens(h_hbm, EXPERT_BLOCK, xbuf.at[slot], sem.at[slot])

        o_ref[...] = jnp.zeros(o_ref.shape, U32)


def _expert_ffn(block_expert, n_used, src, h, w_gate, w_up, w_down, *, layer):
    nb = block_expert.shape[0]
    d = D_MODEL
    blk_rows = EXPERT_BLOCK * TOKEN_PITCH
    nxt = lambda b, be, nu: (jnp.minimum(b + 1, nb - 1), 0, 0)
    wsel = lambda b, be, nu: (layer * N_EXPERTS + be[b], 0, 0)
    grid_spec = pltpu.PrefetchScalarGridSpec(
        num_scalar_prefetch=2,
        grid=(nb,),
        in_specs=[pl.BlockSpec((1, 1, EXPERT_BLOCK), lambda b, be, nu: (b, 0, 0), memory_space=pltpu.SMEM),
                  pl.BlockSpec((1, 1, EXPERT_BLOCK), nxt, memory_space=pltpu.SMEM),
                  pl.BlockSpec(memory_space=pl.ANY),
                  pl.BlockSpec((1, d, D_EXPERT), wsel),
                  pl.BlockSpec((1, d, D_EXPERT), wsel),
                  pl.BlockSpec((1, D_EXPERT, d), wsel)],
        out_specs=pl.BlockSpec((blk_rows, LANES), lambda b, be, nu: (b, 0)),
        scratch_shapes=[pltpu.VMEM((2, blk_rows, LANES), U32),
                        pltpu.SemaphoreType.DMA((2,)),
                        pltpu.VMEM((EXPERT_BLOCK, d), BF16),
                        pltpu.VMEM((EXPERT_BLOCK, D_EXPERT), BF16),
                        pltpu.VMEM((d, D_EXPERT), BF16),
                        pltpu.VMEM((d, D_EXPERT), BF16),
                        pltpu.VMEM((D_EXPERT, d), BF16)])
    return pl.pallas_call(
        _expert_kernel,
        grid_spec=grid_spec,
        out_shape=jax.ShapeDtypeStruct((nb * blk_rows, LANES), U32),
        compiler_params=_cparams(1, disable_bounds_checks=True),
        name="expert_ffn",
    )(block_expert, n_used, src, src, h, w_gate, w_up, w_down)


def _combine_kernel(pos_cur, pos_nxt, x_ref, route_ref, fg_ref, y_hbm, o_ref, ybuf, sem, *, apply_final_norm):
    i = pl.program_id(0)
    n = pl.num_programs(0)
    tm = x_ref.shape[0]
    slot = i % 2
    per_issue = TOP_K * tm // TOKEN_WORDS

    @pl.when(i == 0)
    def _():
        _gather_tokens_loop(pos_cur, TOP_K * tm, y_hbm, ybuf.at[0], sem.at[0])

    _wait_tokens(y_hbm, TOP_K * tm, ybuf.at[slot], sem.at[slot])
    r = route_ref[...]
    g0 = r[:, 4:5]
    g1 = r[:, 5:6]
    for c in range(TOKEN_WORDS):
        first = _load_token_chunks(ybuf.at[slot], 0, tm, c)
        second = _load_token_chunks(ybuf.at[slot], tm, tm, c)
        _gather_tokens(pos_nxt, c * per_issue, (c + 1) * per_issue, y_hbm, ybuf.at[1 - slot], sem.at[1 - slot])
        for half in range(2):
            cs_ = slice((c + half * TOKEN_WORDS) * LANES, (c + half * TOKEN_WORDS + 1) * LANES)
            o_ref[:, cs_] = x_ref[:, cs_] + (g0 * first[half] + g1 * second[half])
    if apply_final_norm:
        o_ref[...] = _rms(o_ref[...], fg_ref[...])

    @pl.when(i == n - 1)
    def _():
        _wait_tokens(y_hbm, TOP_K * tm, ybuf.at[1 - slot], sem.at[1 - slot])


def _combine(pos, x, route, yb, final_g, *, apply_final_norm, tm=512):
    m, d = x.shape
    nt = m // tm
    return pl.pallas_call(
        functools.partial(_combine_kernel, apply_final_norm=apply_final_norm),
        grid=(nt,),
        in_specs=[pl.BlockSpec((1, 1, TOP_K * tm), lambda i: (i, 0, 0), memory_space=pltpu.SMEM),
                  pl.BlockSpec((1, 1, TOP_K * tm), lambda i: (jnp.minimum(i + 1, nt - 1), 0, 0),
                               memory_space=pltpu.SMEM),
                  pl.BlockSpec((tm, d), lambda i: (i, 0)),
                  pl.BlockSpec((tm, ROUTE_LANES), lambda i: (i, 0)),
                  pl.BlockSpec((1, d), lambda i: (0, 0)),
                  pl.BlockSpec(memory_space=pl.ANY)],
        out_specs=pl.BlockSpec((tm, d), lambda i: (i, 0)),
        out_shape=jax.ShapeDtypeStruct((m, d), F32),
        scratch_shapes=[pltpu.VMEM((2, TOP_K * tm * TOKEN_PITCH, LANES), U32),
                        pltpu.SemaphoreType.DMA((2,))],
        compiler_params=_cparams(1, disable_bounds_checks=True),
        name="combine",
    )(pos, pos, x, route, final_g, yb)


def _hier_moe(x, g, wr, br, w_gate, w_up, w_down, final_g, *, layer, apply_final_norm, combine_tm=512):
    n_tok = x.shape[0]
    h, route, cnt = _router(x, g, wr, br)
    counts = cnt[0, EXPERT_LANE0:EXPERT_LANE0 + N_EXPERTS].astype(I32)
    nblk = (counts + EXPERT_BLOCK - 1) // EXPERT_BLOCK
    bend = jnp.cumsum(nblk)
    bstart = bend - nblk
    nb = n_tok * TOP_K // EXPERT_BLOCK + N_EXPERTS
    choice = route[:, 0:2 * TOP_K].T.astype(I32)
    expert = choice[0:TOP_K]
    rank = choice[TOP_K:2 * TOP_K]
    ids = jnp.arange(N_EXPERTS, dtype=I32)[:, None, None]
    first_row = jnp.sum(jnp.where(expert[None] == ids, bstart[:, None, None] * EXPERT_BLOCK, 0), axis=0)
    pos = first_row + rank
    tok = jnp.broadcast_to(jnp.arange(n_tok, dtype=I32)[None, :], pos.shape)
    src = jnp.zeros((nb * EXPERT_BLOCK,), I32).at[pos.reshape(-1)].set(tok.reshape(-1))
    block_expert = jnp.sum((jnp.arange(nb, dtype=I32)[:, None] >= bend[None, :]).astype(I32), axis=1)
    block_expert = jnp.minimum(block_expert, N_EXPERTS - 1)
    yb = _expert_ffn(block_expert, bend[-1:].astype(I32), src.reshape(nb, 1, EXPERT_BLOCK),
                     h, w_gate, w_up, w_down, layer=layer)
    pos_tiles = pos.reshape(TOP_K, n_tok // combine_tm, combine_tm).transpose(1, 0, 2)
    pos_tiles = pos_tiles.reshape(n_tok // combine_tm, 1, TOP_K * combine_tm)
    return _combine(pos_tiles, x, route, yb, final_g, apply_final_norm=apply_final_norm, tm=combine_tm)


def _rope_tables(seqlen):
    inv_freq = ROPE_THETA ** (-jnp.arange(0, QK_ROPE, 2, dtype=F32) / QK_ROPE)
    ang = jnp.arange(seqlen, dtype=F32)[:, None] * inv_freq[None, :]
    cos, sin = jnp.cos(ang), jnp.sin(ang)
    zero = jnp.zeros((seqlen, LANES - QK_ROPE), F32)
    return (jnp.concatenate([cos, cos, zero], axis=-1), jnp.concatenate([-sin, sin, zero], axis=-1))


def _swap_halves(w):
    half = w.shape[-1] // 2
    return jnp.concatenate([w[..., half:], w[..., :half]], axis=-1)


def _router_weights(rg_w, rg_b, re_w, re_b):
    d = rg_w.shape[0]
    w = jnp.concatenate([rg_w, re_w.transpose(1, 0, 2).reshape(d, N_EXPERTS)], axis=1)
    b = jnp.concatenate([rg_b, re_b.reshape(N_EXPERTS)])
    pad = ROUTE_LANES - w.shape[1]
    w = jnp.pad(w, ((0, 0), (0, pad)))
    w_hi = w.astype(BF16)
    w_lo = (w - w_hi.astype(F32)).astype(BF16)
    return jnp.concatenate([w_hi, w_lo], axis=1), jnp.pad(b, (0, pad)).reshape(1, ROUTE_LANES)


def _row(v, width=None):
    v = v.reshape(1, -1).astype(F32)
    if width is not None:
        v = jnp.pad(v, ((0, 0), (0, width - v.shape[1])))
    return v


def kernel(x, ssm_norm_g, ssm_in_w, ssm_conv_w, ssm_conv_b, ssm_dt_bias, ssm_A_log, ssm_D, ssm_gate_norm_g, ssm_out_w, kv_norm_g, kv_w_dkv, kv_latent_g, kv_w_uk, kv_w_uv, kv_w_kr, attn_norm_g, q_w_dq, q_latent_g, q_w_uq, attn_w_o, ffn_norm_g, router_group_w, router_group_b, router_expert_w, router_expert_b, expert_w_gate, expert_w_up, expert_w_down, final_norm_g):
    bsz, seqlen, d = x.shape
    n_tok = bsz * seqlen
    xs = x.reshape(n_tok, d)
    n_ssm = ssm_in_w.shape[0]
    depth = ffn_norm_g.shape[0]
    ck, sk = _rope_tables(seqlen)
    zx_cols = D_INNER + CONV_DIM
    k_all = v_all = None
    w_in_all = ssm_in_w.astype(BF16)
    w_gate_all = expert_w_gate.reshape(depth * N_EXPERTS, d, D_EXPERT)
    w_up_all = expert_w_up.reshape(depth * N_EXPERTS, d, D_EXPERT)
    w_down_all = expert_w_down.reshape(depth * N_EXPERTS, D_EXPERT, d)

    for layer in range(depth):
        if layer < n_ssm:
            i = layer
            w_dt = jnp.pad(w_in_all[i, :, zx_cols:], ((0, 0), (0, LANES - SSM_HEADS)))
            zx, dt_raw = _in_proj(xs, _row(ssm_norm_g[i]), w_in_all, w_dt, layer=i)
            yn = _ssd(zx, dt_raw, ssm_conv_w[i], _row(ssm_conv_b[i]), _row(ssm_dt_bias[i], LANES),
                      _row(ssm_A_log[i], LANES), _row(jnp.repeat(ssm_D[i], SSM_HEADDIM)),
                      _row(ssm_gate_norm_g[i]), bsz=bsz, seqlen=seqlen)
            xs = _mm_res(yn, ssm_out_w[i].astype(BF16), xs)
        else:
            j = layer - n_ssm
            if k_all is None:
                wkr2 = jnp.concatenate([kv_w_kr, _swap_halves(kv_w_kr)], axis=1).astype(BF16)
                k_all, v_all = _shared_kv(xs, _row(kv_norm_g), kv_w_dkv.astype(BF16), _row(kv_latent_g),
                                          kv_w_uk.astype(BF16), kv_w_uv.astype(BF16), wkr2, ck, sk,
                                          seqlen=seqlen)
            wq = q_w_uq[j].reshape(Q_LORA, MLA_HEADS, QK_NOPE + QK_ROPE)
            wn = wq[:, :, :QK_NOPE].reshape(Q_LORA, MLA_HEADS * QK_NOPE).astype(BF16)
            wr = wq[:, :, QK_NOPE:]
            lane_pad = ((0, 0), (0, 0), (0, LANES - QK_ROPE))
            wr_p = jnp.pad(wr, lane_pad).reshape(Q_LORA, MLA_HEADS * LANES).astype(BF16)
            ws_p = jnp.pad(_swap_halves(wr), lane_pad).reshape(Q_LORA, MLA_HEADS * LANES).astype(BF16)
            q_all = _q_side(xs, _row(attn_norm_g[j]), q_w_dq[j].astype(BF16), _row(q_latent_g[j]),
                            wn, wr_p, ws_p, ck, sk, seqlen=seqlen)
            o = _attention(q_all, k_all, v_all, bsz=bsz, seqlen=seqlen)
            xs = _mm_res(o, attn_w_o[j].astype(BF16), xs)
        wr_l, br_l = _router_weights(router_group_w[layer], router_group_b[layer],
                                     router_expert_w[layer], router_expert_b[layer])
        xs = _hier_moe(xs, _row(ffn_norm_g[layer]), wr_l, br_l, w_gate_all, w_up_all, w_down_all,
                       _row(final_norm_g), layer=layer, apply_final_norm=(layer == depth - 1))
    return xs.reshape(bsz, seqlen, d)
```
